```python
import math
import jax
import jax.numpy as jnp
from jax import lax
import numpy as np

D_MODEL = 2048
BATCH = 4
SEQ = 4096
DEPTH = 2

D_MIX = D_MODEL
GROUP_W = D_MIX // 4
ROPE_THETA = 500000.0
NORM_EPS = 1e-6
Q_BLOCK = 128

MLA_HEADS = 4
MLA_NOPE = 128
MLA_ROPE = 64
MLA_V = 128
MLA_Q_RANK = 384
MLA_KV_RANK = 256

S5_CH = GROUP_W
S5_GROUP = 16
S5_NGROUPS = S5_CH // S5_GROUP
S5_STATE = 64

NSA_HEADS = 4
NSA_DIM = 128
NSA_ROT = NSA_DIM // 4
CMP_BLOCK = 32
CMP_STRIDE = 16
SEL_BLOCK = 64
SEL_TOPK = 16
WINDOW = 512

GDN_HEADS = 4
GDN_DIM = 128
GDN_CONV = 4
GDN_CHUNK = 64

D_FF = 5632
FFN_CONV = 3

IN_SIZES = (MLA_Q_RANK, MLA_KV_RANK, MLA_ROPE,
            S5_CH,
            NSA_HEADS * NSA_DIM, NSA_DIM, NSA_DIM, NSA_DIM, NSA_DIM, NSA_DIM, NSA_DIM, 3 * NSA_HEADS,
            GDN_HEADS * GDN_DIM, GDN_HEADS * GDN_DIM, GDN_HEADS * GDN_DIM, GDN_HEADS * GDN_DIM, GDN_HEADS, GDN_HEADS)
IN_COLS = sum(IN_SIZES)
MIX_OUT = MLA_HEADS * MLA_V + S5_CH + NSA_HEADS * NSA_DIM + GDN_HEADS * GDN_DIM

kernel_name = 'hymba_parallel_mla_s5_nsa_gdn_convffn'


def rms_norm(x, g):
    xf = x.astype(jnp.float32)
    y = xf * lax.rsqrt(jnp.mean(xf * xf, axis=-1, keepdims=True) + NORM_EPS)
    return (y * g.astype(jnp.float32)).astype(x.dtype)


def l2_norm(x):
    return x * lax.rsqrt(jnp.sum(x * x, axis=-1, keepdims=True) + 1e-6)


def rope(x, pos, rot_dim):
    half = rot_dim // 2
    inv_freq = ROPE_THETA ** (-jnp.arange(half, dtype=jnp.float32) / half)
    ang = pos.astype(jnp.float32)[:, :, None] * inv_freq
    cos = jnp.cos(ang)[:, :, None, :]
    sin = jnp.sin(ang)[:, :, None, :]
    xr = x[..., :rot_dim].astype(jnp.float32)
    x1, x2 = xr[..., :half], xr[..., half:]
    rot = jnp.concatenate([x1 * cos - x2 * sin, x2 * cos + x1 * sin], axis=-1).astype(x.dtype)
    return jnp.concatenate([rot, x[..., rot_dim:]], axis=-1)


def causal_dwconv(x, w):
    k = w.shape[0]
    return lax.conv_general_dilated(x, w[:, None, :].astype(x.dtype), window_strides=(1,),
                                    padding=[(k - 1, 0)], dimension_numbers=('NWC', 'WIO', 'NWC'),
                                    feature_group_count=x.shape[-1])


def split_cols(x, sizes):
    return jnp.split(x, np.cumsum(sizes)[:-1].tolist(), axis=-1)


def blocked_causal_attention(q, k, v, scale):
    B, S, H, Dk = q.shape
    nb = S // Q_BLOCK
    qb = q.reshape(B, nb, Q_BLOCK, H, Dk).swapaxes(0, 1)
    kpos = jnp.arange(S)

    def one(args):
        i, qi = args
        s = jnp.einsum('bqhd,bkhd->bhqk', qi, k).astype(jnp.float32) * scale
        qpos = i * Q_BLOCK + jnp.arange(Q_BLOCK)
        s = jnp.where(kpos[None, :] <= qpos[:, None], s, -jnp.inf)
        p = jax.nn.softmax(s, axis=-1).astype(v.dtype)
        return jnp.einsum('bhqk,bkhd->bqhd', p, v)

    o = lax.map(one, (jnp.arange(nb), qb))
    return o.swapaxes(0, 1).reshape(B, S, H, v.shape[-1])


def mla_mixer(cq, ckv, kpe, pos, q_norm, w_uq, kv_norm, w_ukv):
    B, S, _ = cq.shape
    q = (rms_norm(cq, q_norm) @ w_uq).reshape(B, S, MLA_HEADS, MLA_NOPE + MLA_ROPE)
    kv = (rms_norm(ckv, kv_norm) @ w_ukv).reshape(B, S, MLA_HEADS, MLA_NOPE + MLA_V)
    q = jnp.concatenate([q[..., :MLA_NOPE], rope(q[..., MLA_NOPE:], pos, MLA_ROPE)], axis=-1)
    k_pe = rope(kpe[:, :, None, :], pos, MLA_ROPE)
    k = jnp.concatenate([kv[..., :MLA_NOPE], jnp.broadcast_to(k_pe, (B, S, MLA_HEADS, MLA_ROPE))], axis=-1)
    o = blocked_causal_attention(q, k, kv[..., MLA_NOPE:], (MLA_NOPE + MLA_ROPE) ** -0.5)
    return o.reshape(B, S, MLA_HEADS * MLA_V)


def s5_mixer(u, a_re, a_im, b_re, b_im, c_re, c_im, d_skip, log_step, w_glu, b_glu):
    B, S, _ = u.shape
    f32 = jnp.float32
    uf = u.astype(f32).reshape(B, S, S5_NGROUPS, S5_GROUP)
    step = jnp.exp(log_step.astype(f32))[:, None]
    are, aim = a_re.astype(f32), a_im.astype(f32)
    mag = jnp.exp(are * step)
    lb_re, lb_im = mag * jnp.cos(aim * step), mag * jnp.sin(aim * step)
    den = are * are + aim * aim
    nr, ni = lb_re - 1.0, lb_im
    g_re = (nr * are + ni * aim) / den
    g_im = (ni * are - nr * aim) / den
    br, bi = b_re.astype(f32), b_im.astype(f32)
    bb_re = g_re[..., None] * br - g_im[..., None] * bi
    bb_im = g_re[..., None] * bi + g_im[..., None] * br
    bu_re = jnp.einsum('bsgc,gpc->bsgp', uf, bb_re)
    bu_im = jnp.einsum('bsgc,gpc->bsgp', uf, bb_im)
    lam_re = jnp.broadcast_to(lb_re, bu_re.shape)
    lam_im = jnp.broadcast_to(lb_im, bu_re.shape)

    def combine(e1, e2):
        a1r, a1i, b1r, b1i = e1
        a2r, a2i, b2r, b2i = e2
        return (a2r * a1r - a2i * a1i, a2r * a1i + a2i * a1r,
                a2r * b1r - a2i * b1i + b2r, a2r * b1i + a2i * b1r + b2i)

    _, _, xr, xi = lax.associative_scan(combine, (lam_re, lam_im, bu_re, bu_im), axis=1)
    y = jnp.einsum('bsgp,gcp->bsgc', xr, c_re.astype(f32)) - jnp.einsum('bsgp,gcp->bsgc', xi, c_im.astype(f32))
    y = y.reshape(B, S, S5_CH) + d_skip.astype(f32) * u.astype(f32)
    y = jax.nn.gelu(y).astype(u.dtype)
    y = y * jax.nn.sigmoid(y @ w_glu + b_glu)
    return y


def compress_blocks(xb, pos_emb, w1, w2):
    B, N, L, D = xb.shape
    h = jax.nn.gelu((xb + pos_emb).reshape(B, N, L * D) @ w1)
    return h @ w2


def nsa_compressed(q, kc, vc, pos_k, w1k, w2k, pos_v, w1v, w2v, scale):
    B, S, H, D = q.shape
    n_cmp = (S - CMP_BLOCK) // CMP_STRIDE + 1
    idx = np.arange(n_cmp)[:, None] * CMP_STRIDE + np.arange(CMP_BLOCK)[None, :]
    k_cmp = compress_blocks(kc[:, idx], pos_k, w1k, w2k)
    v_cmp = compress_blocks(vc[:, idx], pos_v, w1v, w2v)
    s = jnp.einsum('bshd,bnd->bhsn', q, k_cmp).astype(jnp.float32) * scale
    blk_end = np.arange(n_cmp) * CMP_STRIDE + CMP_BLOCK - 1
    valid = jnp.arange(S)[:, None] >= blk_end[None, :]
    p = jax.nn.softmax(jnp.where(valid, s, -1e30), axis=-1)
    p = jnp.where(valid, p, 0.0)
    o = jnp.einsum('bhsn,bnd->bshd', p.astype(v_cmp.dtype), v_cmp)
    return o, p


def nsa_select_blocks(p_cmp):
    B, H, S, n_cmp = p_cmp.shape
    n_sel = S // SEL_BLOCK
    sel_start = np.arange(n_sel) * SEL_BLOCK
    cmp_start = np.arange(n_cmp) * CMP_STRIDE
    overlap = ((cmp_start[:, None] < sel_start[None, :] + SEL_BLOCK)
               & (cmp_start[:, None] + CMP_BLOCK > sel_start[None, :])).astype(np.float32)
    imp = jnp.einsum('bhsn,nj->bsj', p_cmp, jnp.asarray(overlap))
    t_blk = jnp.arange(S) // SEL_BLOCK
    j = jnp.arange(n_sel)
    forced = (j[None, :] == 0) | (j[None, :] == t_blk[:, None]) | (j[None, :] == t_blk[:, None] - 1)
    future = j[None, :] > t_blk[:, None]
    imp = jnp.where(forced, 1e6, jnp.where(future, -1e6, imp))
    _, sel_idx = lax.top_k(imp, min(SEL_TOPK, n_sel))
    return sel_idx


def nsa_selected(q, ks, vs, sel_idx, scale):
    B, S, H, D = q.shape
    n_sel = S // SEL_BLOCK
    nb = S // Q_BLOCK
    n_top = sel_idx.shape[-1]
    ks_b = ks.reshape(B, n_sel, SEL_BLOCK, D)
    vs_b = vs.reshape(B, n_sel, SEL_BLOCK, D)
    qb = q.reshape(B, nb, Q_BLOCK, H, D).swapaxes(0, 1)
    ib = sel_idx.reshape(B, nb, Q_BLOCK, n_top).swapaxes(0, 1)
    bidx = jnp.arange(B)[:, None, None]

    def one(args):
        i, qi, ii = args
        kg = ks_b[bidx, ii]
        vg = vs_b[bidx, ii]
        s = jnp.einsum('bqhd,bqkld->bhqkl', qi, kg).astype(jnp.float32) * scale
        kpos = ii[..., None] * SEL_BLOCK + jnp.arange(SEL_BLOCK)
        qpos = i * Q_BLOCK + jnp.arange(Q_BLOCK)
        mask = kpos <= qpos[None, :, None, None]
        s = jnp.where(mask[:, None], s, -jnp.inf)
        p = jax.nn.softmax(s.reshape(B, H, Q_BLOCK, n_top * SEL_BLOCK), axis=-1).reshape(s.shape)
        return jnp.einsum('bhqkl,bqkld->bqhd', p.astype(vg.dtype), vg)

    o = lax.map(one, (jnp.arange(nb), qb, ib))
    return o.swapaxes(0, 1).reshape(B, S, H, D)


def nsa_window(q, kw, vw, scale):
    B, S, H, D = q.shape
    nb = S // Q_BLOCK
    kp = jnp.pad(kw, ((0, 0), (WINDOW, 0), (0, 0)))
    vp = jnp.pad(vw, ((0, 0), (WINDOW, 0), (0, 0)))
    qb = q.reshape(B, nb, Q_BLOCK, H, D).swapaxes(0, 1)

    def one(args):
        i, qi = args
        kb = lax.dynamic_slice_in_dim(kp, i * Q_BLOCK, WINDOW + Q_BLOCK, axis=1)
        vb = lax.dynamic_slice_in_dim(vp, i * Q_BLOCK, WINDOW + Q_BLOCK, axis=1)
        s = jnp.einsum('bqhd,bkd->bhqk', qi, kb).astype(jnp.float32) * scale
        kpos = i * Q_BLOCK - WINDOW + jnp.arange(WINDOW + Q_BLOCK)
        qpos = i * Q_BLOCK + jnp.arange(Q_BLOCK)
        rel = qpos[:, None] - kpos[None, :]
        mask = (rel >= 0) & (rel < WINDOW) & (kpos[None, :] >= 0)
        p = jax.nn.softmax(jnp.where(mask, s, -jnp.inf), axis=-1)
        return jnp.einsum('bhqk,bkd->bqhd', p.astype(vb.dtype), vb)

    o = lax.map(one, (jnp.arange(nb), qb))
    return o.swapaxes(0, 1).reshape(B, S, H, D)


def nsa_mixer(nq, kc, vc, ks, vs, kw, vw, ngate, pos, pos_k, w1k, w2k, pos_v, w1v, w2v):
    B, S, _ = nq.shape
    scale = NSA_DIM ** -0.5
    q = rope(nq.reshape(B, S, NSA_HEADS, NSA_DIM), pos, NSA_ROT)

    def rot_k(t):
        return rope(t[:, :, None, :], pos, NSA_ROT)[:, :, 0, :]

    o_cmp, p_cmp = nsa_compressed(q, rot_k(kc), vc, pos_k, w1k, w2k, pos_v, w1v, w2v, scale)
    sel_idx = nsa_select_blocks(p_cmp)
    o_slc = nsa_selected(q, rot_k(ks), vs, sel_idx, scale)
    o_win = nsa_window(q, rot_k(kw), vw, scale)
    gates = jax.nn.sigmoid(ngate).reshape(B, S, 3, NSA_HEADS, 1)
    o = gates[:, :, 0] * o_cmp + gates[:, :, 1] * o_slc + gates[:, :, 2] * o_win
    return o.reshape(B, S, NSA_HEADS * NSA_DIM)


def chunked_gated_delta_rule(q, k, v, g, beta):
    B, S, H, D = q.shape
    C = GDN_CHUNK
    N = S // C

    def chunks(t):
        return t.reshape(B, N, C, H, -1).transpose(1, 0, 3, 2, 4)

    qc, kc, vc = chunks(q), chunks(k), chunks(v)
    gc = jnp.cumsum(g.reshape(B, N, C, H).transpose(1, 0, 3, 2), axis=-1)
    bc = beta.reshape(B, N, C, H).transpose(1, 0, 3, 2)
    incl = jnp.tril(jnp.ones((C, C), dtype=bool))
    strict = jnp.tril(jnp.ones((C, C), dtype=bool), -1)
    diff = gc[..., :, None] - gc[..., None, :]
    decay = jnp.where(incl, jnp.exp(jnp.where(incl, diff, 0.0)), 0.0)
    k_beta = kc * bc[..., None]
    v_beta = vc * bc[..., None]
    lower = jnp.where(strict, jnp.einsum('nbhid,nbhjd->nbhij', k_beta, kc) * decay, 0.0)
    eye = jnp.eye(C, dtype=q.dtype)
    t_inv = lax.linalg.triangular_solve(eye + lower, jnp.broadcast_to(eye, lower.shape),
                                        left_side=True, lower=True)
    u = t_inv @ v_beta
    w = t_inv @ (k_beta * jnp.exp(gc)[..., None])
    intra = jnp.einsum('nbhid,nbhjd->nbhij', qc, kc) * decay
    q_dec = qc * jnp.exp(gc)[..., None]
    g_last = gc[..., -1]
    k_dec = kc * jnp.exp(g_last[..., None] - gc)[..., None]

    def step(state, xs):
        q_i, u_i, w_i, a_i, k_i, gl_i = xs
        v_new = u_i - w_i @ state
        o_i = q_i @ state + a_i @ v_new
        state = state * jnp.exp(gl_i)[..., None, None] + jnp.swapaxes(k_i, -1, -2) @ v_new
        return state, o_i

    state0 = jnp.zeros((B, H, D, D), q.dtype)
    _, o = lax.scan(step, state0, (q_dec, u, w, intra, k_dec, g_last))
    return o.transpose(1, 0, 3, 2, 4).reshape(B, S, H, D)


def gdn_mixer(gq, gk, gv, gz, ga, gb, conv_w, a_log, dt_bias, o_norm):
    B, S, _ = gq.shape
    f32 = jnp.float32
    qkv = jax.nn.silu(causal_dwconv(jnp.concatenate([gq, gk, gv], axis=-1), conv_w))
    q, k, v = [t.reshape(B, S, GDN_HEADS, GDN_DIM).astype(f32) for t in jnp.split(qkv, 3, axis=-1)]
    q = l2_norm(q) * GDN_DIM ** -0.5
    k = l2_norm(k)
    beta = jax.nn.sigmoid(gb.astype(f32))
    g = -jnp.exp(a_log.astype(f32)) * jax.nn.softplus(ga.astype(f32) + dt_bias.astype(f32))
    o = chunked_gated_delta_rule(q, k, v, g, beta)
    o = rms_norm(o, o_norm) * jax.nn.silu(gz.reshape(B, S, GDN_HEADS, GDN_DIM).astype(f32))
    return o.reshape(B, S, GDN_HEADS * GDN_DIM).astype(gq.dtype)


def token_mixing(h, pos, w_in, w_out, gn_mla, gn_s5, gn_nsa,
                 mla_q_norm, mla_w_uq, mla_kv_norm, mla_w_ukv,
                 s5_a_re, s5_a_im, s5_b_re, s5_b_im, s5_c_re, s5_c_im, s5_d, s5_log_step, s5_w_glu, s5_b_glu,
                 nsa_pos_k, nsa_w1_k, nsa_w2_k, nsa_pos_v, nsa_w1_v, nsa_w2_v,
                 gdn_conv_w, gdn_a_log, gdn_dt_bias, gdn_o_norm):
    proj = h @ w_in
    (cq, ckv, kpe, u_s5, nq, kc, vc, ks, vs, kw, vw, ngate,
     gq, gk, gv, gz, ga, gb) = split_cols(proj, IN_SIZES)
    o_a = mla_mixer(cq, ckv, kpe, pos, mla_q_norm, mla_w_uq, mla_kv_norm, mla_w_ukv)
    o_b = s5_mixer(u_s5, s5_a_re, s5_a_im, s5_b_re, s5_b_im, s5_c_re, s5_c_im, s5_d, s5_log_step,
                   s5_w_glu, s5_b_glu)
    o_c = nsa_mixer(nq, kc, vc, ks, vs, kw, vw, ngate, pos, nsa_pos_k, nsa_w1_k, nsa_w2_k,
                    nsa_pos_v, nsa_w1_v, nsa_w2_v)
    o_d = gdn_mixer(gq, gk, gv, gz, ga, gb, gdn_conv_w, gdn_a_log, gdn_dt_bias, gdn_o_norm)
    o = jnp.concatenate([rms_norm(o_a, gn_mla), rms_norm(o_b, gn_s5), rms_norm(o_c, gn_nsa), o_d], axis=-1)
    return o @ w_out


def conv_ffn(h, w_in, conv_w, w_out):
    gate, val = jnp.split(h @ w_in, 2, axis=-1)
    gate = causal_dwconv(gate, conv_w)
    return (jax.nn.gelu(gate) * val) @ w_out


def setup_inputs(seed: int = 0) -> dict:
    key = jax.random.key(seed)
    keys = jax.random.split(key, 64)
    counter = [0]
    f32 = jnp.float32
    L = DEPTH

    def nk():
        counter[0] += 1
        return keys[counter[0] - 1]

    def nrm(shape, scale):
        return jax.random.normal(nk(), shape, f32) * scale

    def gain(width):
        return 1.0 + nrm((L, width), 0.02)

    x = nrm((BATCH, SEQ, D_MODEL), 1.0)
    c = nrm((BATCH, D_MODEL), 1.0)
    positions = (jnp.arange(SEQ, dtype=jnp.int32)[None, :]
                 + jax.random.randint(nk(), (BATCH, 1), 0, 1024, dtype=jnp.int32))
    w_ada = nrm((L, D_MODEL, 6 * D_MODEL), D_MODEL ** -0.5)
    b_ada = nrm((L, 6 * D_MODEL), 0.01)
    norm_pre_mix = gain(D_MODEL)
    norm_post_mix = gain(D_MODEL)
    norm_pre_ffn = gain(D_MODEL)
    norm_post_ffn = gain(D_MODEL)
    w_in = nrm((L, D_MODEL, IN_COLS), D_MODEL ** -0.5)
    w_out = nrm((L, MIX_OUT, D_MODEL), MIX_OUT ** -0.5)
    gn_mla = gain(MLA_HEADS * MLA_V)
    gn_s5 = gain(S5_CH)
    gn_nsa = gain(NSA_HEADS * NSA_DIM)
    mla_q_norm = gain(MLA_Q_RANK)
    mla_w_uq = nrm((L, MLA_Q_RANK, MLA_HEADS * (MLA_NOPE + MLA_ROPE)), MLA_Q_RANK ** -0.5)
    mla_kv_norm = gain(MLA_KV_RANK)
    mla_w_ukv = nrm((L, MLA_KV_RANK, MLA_HEADS * (MLA_NOPE + MLA_V)), MLA_KV_RANK ** -0.5)
    s5_a_re = -0.5 + nrm((L, S5_NGROUPS, S5_STATE), 0.01)
    s5_a_im = math.pi * jnp.arange(S5_STATE, dtype=f32) + nrm((L, S5_NGROUPS, S5_STATE), 0.01)
    s5_b_re = nrm((L, S5_NGROUPS, S5_STATE, S5_GROUP), 0.5 ** 0.5)
    s5_b_im = nrm((L, S5_NGROUPS, S5_STATE, S5_GROUP), 0.5 ** 0.5)
    s5_c_re = nrm((L, S5_NGROUPS, S5_GROUP, S5_STATE), (2 * S5_STATE) ** -0.5)
    s5_c_im = nrm((L, S5_NGROUPS, S5_GROUP, S5_STATE), (2 * S5_STATE) ** -0.5)
    s5_d = nrm((L, S5_CH), 1.0)
    s5_log_step = jax.random.uniform(nk(), (L, S5_NGROUPS), f32, math.log(1e-3), math.log(1e-1))
    s5_w_glu = nrm((L, S5_CH, S5_CH), S5_CH ** -0.5)
    s5_b_glu = nrm((L, S5_CH), 0.01)
    nsa_pos_k = nrm((L, CMP_BLOCK, NSA_DIM), 0.02)
    nsa_w1_k = nrm((L, CMP_BLOCK * NSA_DIM, NSA_DIM), (CMP_BLOCK * NSA_DIM) ** -0.5)
    nsa_w2_k = nrm((L, NSA_DIM, NSA_DIM), NSA_DIM ** -0.5)
    nsa_pos_v = nrm((L, CMP_BLOCK, NSA_DIM), 0.02)
    nsa_w1_v = nrm((L, CMP_BLOCK * NSA_DIM, NSA_DIM), (CMP_BLOCK * NSA_DIM) ** -0.5)
    nsa_w2_v = nrm((L, NSA_DIM, NSA_DIM), NSA_DIM ** -0.5)
    gdn_conv_w = nrm((L, GDN_CONV, 3 * GDN_HEADS * GDN_DIM), GDN_CONV ** -0.5)
    gdn_a_log = jnp.log(jax.random.uniform(nk(), (L, GDN_HEADS), f32, 1.0, 16.0))
    dt = jnp.exp(jax.random.uniform(nk(), (L, GDN_HEADS), f32, math.log(1e-3), math.log(1e-1)))
    gdn_dt_bias = dt + jnp.log(-jnp.expm1(-dt))
    gdn_o_norm = gain(GDN_DIM)
    ffn_w_in = nrm((L, D_MODEL, 2 * D_FF), D_MODEL ** -0.5)
    ffn_conv_w = nrm((L, FFN_CONV, D_FF), FFN_CONV ** -0.5)
    ffn_w_out = nrm((L, D_FF, D_MODEL), D_FF ** -0.5)
    return {'x': x, 'c': c, 'positions': positions, 'w_ada': w_ada, 'b_ada': b_ada,
            'norm_pre_mix': norm_pre_mix, 'norm_post_mix': norm_post_mix,
            'norm_pre_ffn': norm_pre_ffn, 'norm_post_ffn': norm_post_ffn,
            'w_in': w_in, 'w_out': w_out, 'gn_mla': gn_mla, 'gn_s5': gn_s5, 'gn_nsa': gn_nsa,
            'mla_q_norm': mla_q_norm, 'mla_w_uq': mla_w_uq, 'mla_kv_norm': mla_kv_norm, 'mla_w_ukv': mla_w_ukv,
            's5_a_re': s5_a_re, 's5_a_im': s5_a_im, 's5_b_re': s5_b_re, 's5_b_im': s5_b_im,
            's5_c_re': s5_c_re, 's5_c_im': s5_c_im, 's5_d': s5_d, 's5_log_step': s5_log_step,
            's5_w_glu': s5_w_glu, 's5_b_glu': s5_b_glu,
            'nsa_pos_k': nsa_pos_k, 'nsa_w1_k': nsa_w1_k, 'nsa_w2_k': nsa_w2_k,
            'nsa_pos_v': nsa_pos_v, 'nsa_w1_v': nsa_w1_v, 'nsa_w2_v': nsa_w2_v,
            'gdn_conv_w': gdn_conv_w, 'gdn_a_log': gdn_a_log, 'gdn_dt_bias': gdn_dt_bias, 'gdn_o_norm': gdn_o_norm,
            'ffn_w_in': ffn_w_in, 'ffn_conv_w': ffn_conv_w, 'ffn_w_out': ffn_w_out}


def reference(x, c, positions, w_ada, b_ada, norm_pre_mix, norm_post_mix, norm_pre_ffn, norm_post_ffn,
              w_in, w_out, gn_mla, gn_s5, gn_nsa, mla_q_norm, mla_w_uq, mla_kv_norm, mla_w_ukv,
              s5_a_re, s5_a_im, s5_b_re, s5_b_im, s5_c_re, s5_c_im, s5_d, s5_log_step, s5_w_glu, s5_b_glu,
              nsa_pos_k, nsa_w1_k, nsa_w2_k, nsa_pos_v, nsa_w1_v, nsa_w2_v,
              gdn_conv_w, gdn_a_log, gdn_dt_bias, gdn_o_norm, ffn_w_in, ffn_conv_w, ffn_w_out):
    cond = jax.nn.silu(c)
    for l in range(DEPTH):
        mods = cond @ w_ada[l] + b_ada[l]
        sh1, sc1, g1, sh2, sc2, g2 = [m[:, None, :] for m in jnp.split(mods, 6, axis=-1)]
        h = rms_norm(x, norm_pre_mix[l]) * (1.0 + sc1) + sh1
        y = token_mixing(h, positions, w_in[l], w_out[l], gn_mla[l], gn_s5[l], gn_nsa[l],
                         mla_q_norm[l], mla_w_uq[l], mla_kv_norm[l], mla_w_ukv[l],
                         s5_a_re[l], s5_a_im[l], s5_b_re[l], s5_b_im[l], s5_c_re[l], s5_c_im[l],
                         s5_d[l], s5_log_step[l], s5_w_glu[l], s5_b_glu[l],
                         nsa_pos_k[l], nsa_w1_k[l], nsa_w2_k[l], nsa_pos_v[l], nsa_w1_v[l], nsa_w2_v[l],
                         gdn_conv_w[l], gdn_a_log[l], gdn_dt_bias[l], gdn_o_norm[l])
        x = x + g1 * rms_norm(y, norm_post_mix[l])
        h = rms_norm(x, norm_pre_ffn[l]) * (1.0 + sc2) + sh2
        y = conv_ffn(h, ffn_w_in[l], ffn_conv_w[l], ffn_w_out[l])
        x = x + g2 * rms_norm(y, norm_post_ffn[l])
    return x
```

```python
import functools
import math

import jax
import jax.numpy as jnp
import numpy as np
from jax import lax
from jax.experimental import pallas as pl
from jax.experimental.pallas import tpu as pltpu

F32 = jnp.float32
BF16 = jnp.bfloat16

NORM_EPS = 1e-6
ROPE_THETA = 500000.0
Q_BLOCK = 128

MLA_HEADS, MLA_NOPE, MLA_ROPE, MLA_V, MLA_Q_RANK, MLA_KV_RANK = 4, 128, 64, 128, 384, 256
S5_GROUP, S5_STATE = 16, 64
NSA_HEADS, NSA_DIM = 4, 128
NSA_ROT = NSA_DIM // 4
CMP_BLOCK, CMP_STRIDE, SEL_BLOCK, SEL_TOPK, WINDOW = 32, 16, 64, 16, 512
GDN_HEADS, GDN_DIM, GDN_CONV, GDN_CHUNK = 4, 128, 4, 64

VMEM_LIMIT_BYTES = 56 * 1024 * 1024
HALO_ROWS = 16


def _cparams(*sem):
    return pltpu.CompilerParams(dimension_semantics=sem, vmem_limit_bytes=VMEM_LIMIT_BYTES)


def _ada_kernel(c_ref, w_ref, b_ref, o_ref):
    cond = jax.nn.silu(c_ref[...]).astype(BF16)
    o_ref[...] = jnp.dot(cond, w_ref[...].astype(BF16), preferred_element_type=F32) + b_ref[...]


def ada_mods(c, w_ada, b_ada):
    B, D = c.shape
    N = w_ada.shape[1]
    BP = 8
    c_pad = jnp.zeros((BP, D), F32).at[:B].set(c)
    tn = 1024
    assert N % tn == 0
    out = pl.pallas_call(
        _ada_kernel,
        grid=(N // tn,),
        in_specs=[pl.BlockSpec((BP, D), lambda j: (0, 0)),
                  pl.BlockSpec((D, tn), lambda j: (0, j)),
                  pl.BlockSpec((1, tn), lambda j: (0, j))],
        out_specs=pl.BlockSpec((BP, tn), lambda j: (0, j)),
        out_shape=jax.ShapeDtypeStruct((BP, N), F32),
        compiler_params=_cparams("arbitrary"),
        name="ada_mods",
    )(c_pad, w_ada, b_ada.reshape(1, N))
    return out[:B]


def _normmod_matmul_kernel(x_ref, g_ref, sc_ref, sh_ref, w_ref, o_ref, h_ref):
    @pl.when(pl.program_id(2) == 0)
    def _():
        x = x_ref[0]
        ms = jnp.mean(x * x, axis=-1, keepdims=True)
        y = x * lax.rsqrt(ms + NORM_EPS) * g_ref[...]
        h_ref[...] = (y * (1.0 + sc_ref[0]) + sh_ref[0]).astype(BF16)

    o_ref[0] = jnp.dot(h_ref[...], w_ref[...], preferred_element_type=F32).astype(o_ref.dtype)


def normmod_matmul(x, gain, sc, sh, w_bf16, out_dtype, tm, tn):
    B, S, D = x.shape
    N = w_bf16.shape[1]
    assert S % tm == 0 and N % tn == 0
    return pl.pallas_call(
        _normmod_matmul_kernel,
        grid=(B, S // tm, N // tn),
        in_specs=[pl.BlockSpec((1, tm, D), lambda b, i, j: (b, i, 0)),
                  pl.BlockSpec((1, D), lambda b, i, j: (0, 0)),
                  pl.BlockSpec((1, 1, D), lambda b, i, j: (b, 0, 0)),
                  pl.BlockSpec((1, 1, D), lambda b, i, j: (b, 0, 0)),
                  pl.BlockSpec((D, tn), lambda b, i, j: (0, j))],
        out_specs=pl.BlockSpec((1, tm, tn), lambda b, i, j: (b, i, j)),
        out_shape=jax.ShapeDtypeStruct((B, S, N), out_dtype),
        scratch_shapes=[pltpu.VMEM((tm, D), BF16)],
        compiler_params=_cparams("arbitrary", "arbitrary", "arbitrary"),
        name="normmod_matmul",
    )(x, gain.reshape(1, D), sc.reshape(B, 1, D), sh.reshape(B, 1, D), w_bf16)


def _rms(x, g):
    return x * lax.rsqrt(jnp.mean(x * x, axis=-1, keepdims=True) + NORM_EPS) * g


def _mix_out_kernel(oa_ref, ob_ref, oc_ref, od_ref, gn_ref, w_ref, x_ref, gate_ref, gamma_ref, o_ref):
    gw = oa_ref.shape[-1]
    parts = (_rms(oa_ref[0], gn_ref[0:1, :]), _rms(ob_ref[0], gn_ref[1:2, :]),
             _rms(oc_ref[0], gn_ref[2:3, :]), od_ref[0])
    y = None
    for p, part in enumerate(parts):
        t = jnp.dot(part.astype(BF16), w_ref[p * gw:(p + 1) * gw, :], preferred_element_type=F32)
        y = t if y is None else y + t
    o_ref[0] = x_ref[0] + gate_ref[0] * _rms(y, gamma_ref[...])


def mix_out(o_a, o_b, o_c, o_d, gn3, w_out_bf16, x, gate, gamma, tm):
    B, S, D = x.shape
    gw = o_a.shape[-1]
    grp = pl.BlockSpec((1, tm, gw), lambda b, i: (b, i, 0))
    return pl.pallas_call(
        _mix_out_kernel,
        grid=(B, S // tm),
        in_specs=[grp, grp, grp, grp,
                  pl.BlockSpec((3, gw), lambda b, i: (0, 0)),
                  pl.BlockSpec((4 * gw, D), lambda b, i: (0, 0)),
                  pl.BlockSpec((1, tm, D), lambda b, i: (b, i, 0)),
                  pl.BlockSpec((1, 1, D), lambda b, i: (b, 0, 0)),
                  pl.BlockSpec((1, D), lambda b, i: (0, 0))],
        out_specs=pl.BlockSpec((1, tm, D), lambda b, i: (b, i, 0)),
        out_shape=jax.ShapeDtypeStruct((B, S, D), F32),
        compiler_params=_cparams("arbitrary", "arbitrary"),
        name="mix_out",
    )(o_a, o_b, o_c, o_d, gn3, w_out_bf16, x, gate.reshape(B, 1, D), gamma.reshape(1, D))


def _ffn_out_kernel(gate_ref, halo_ref, val_ref, cw_ref, w_ref, x_ref, g2_ref, gamma_ref, o_ref, acc_ref):
    i, k = pl.program_id(1), pl.program_id(2)
    tm = gate_ref.shape[1]
    g = gate_ref[0].astype(F32)
    halo = jnp.where(i > 0, halo_ref[0].astype(F32), 0.0)
    hl = halo.shape[0]
    row = lax.broadcasted_iota(jnp.int32, g.shape, 0)
    prev1 = jnp.where(row >= 1, pltpu.roll(g, 1, axis=0), halo[hl - 1:hl, :])
    prev2 = jnp.where(row >= 2, pltpu.roll(g, 2, axis=0),
                      jnp.where(row == 1, halo[hl - 1:hl, :], halo[hl - 2:hl - 1, :]))
    conv = cw_ref[0:1, :] * prev2 + cw_ref[1:2, :] * prev1 + cw_ref[2:3, :] * g
    a = (jax.nn.gelu(conv) * val_ref[0].astype(F32)).astype(BF16)
    part = jnp.dot(a, w_ref[...], preferred_element_type=F32)

    @pl.when(k == 0)
    def _():
        acc_ref[...] = part

    @pl.when(k > 0)
    def _():
        acc_ref[...] += part

    @pl.when(k == pl.num_programs(2) - 1)
    def _():
        o_ref[0] = x_ref[0] + g2_ref[0] * _rms(acc_ref[...], gamma_ref[...])


def ffn_out(gv, conv_w, w_out_bf16, x, g2, gamma, tm, tf):
    B, S, D = x.shape
    F = w_out_bf16.shape[0]
    nk = F // tf
    assert F % tf == 0 and S % tm == 0 and tm % HALO_ROWS == 0
    rb = tm // HALO_ROWS
    return pl.pallas_call(
        _ffn_out_kernel,
        grid=(B, S // tm, nk),
        in_specs=[pl.BlockSpec((1, tm, tf), lambda b, i, k: (b, i, k)),
                  pl.BlockSpec((1, HALO_ROWS, tf), lambda b, i, k: (b, jnp.maximum(i * rb - 1, 0), k)),
                  pl.BlockSpec((1, tm, tf), lambda b, i, k: (b, i, k + nk)),
                  pl.BlockSpec((3, tf), lambda b, i, k: (0, k)),
                  pl.BlockSpec((tf, D), lambda b, i, k: (k, 0)),
                  pl.BlockSpec((1, tm, D), lambda b, i, k: (b, i, 0)),
                  pl.BlockSpec((1, 1, D), lambda b, i, k: (b, 0, 0)),
                  pl.BlockSpec((1, D), lambda b, i, k: (0, 0))],
        out_specs=pl.BlockSpec((1, tm, D), lambda b, i, k: (b, i, 0)),
        out_shape=jax.ShapeDtypeStruct((B, S, D), F32),
        scratch_shapes=[pltpu.VMEM((tm, D), F32)],
        compiler_params=_cparams("arbitrary", "arbitrary", "arbitrary"),
        name="ffn_out",
    )(gv, gv, gv, conv_w, w_out_bf16, x, g2.reshape(B, 1, D), gamma.reshape(1, D))


def _j_rms_norm(x, g):
    xf = x.astype(F32)
    y = xf * lax.rsqrt(jnp.mean(xf * xf, axis=-1, keepdims=True) + NORM_EPS)
    return (y * g.astype(F32)).astype(x.dtype)


def _j_l2_norm(x):
    return x * lax.rsqrt(jnp.sum(x * x, axis=-1, keepdims=True) + 1e-6)


def _j_rope(x, pos, rot_dim):
    half = rot_dim // 2
    inv_freq = ROPE_THETA ** (-jnp.arange(half, dtype=F32) / half)
    ang = pos.astype(F32)[:, :, None] * inv_freq
    cos = jnp.cos(ang)[:, :, None, :]
    sin = jnp.sin(ang)[:, :, None, :]
    xr = x[..., :rot_dim].astype(F32)
    x1, x2 = xr[..., :half], xr[..., half:]
    rot = jnp.concatenate([x1 * cos - x2 * sin, x2 * cos + x1 * sin], axis=-1).astype(x.dtype)
    return jnp.concatenate([rot, x[..., rot_dim:]], axis=-1)


def _j_causal_dwconv(x, w):
    k = w.shape[0]
    return lax.conv_general_dilated(x, w[:, None, :].astype(x.dtype), window_strides=(1,),
                                    padding=[(k - 1, 0)], dimension_numbers=('NWC', 'WIO', 'NWC'),
                                    feature_group_count=x.shape[-1])


def _j_blocked_causal_attention(q, k, v, scale):
    B, S, H, Dk = q.shape
    nb = S // Q_BLOCK
    qb = q.reshape(B, nb, Q_BLOCK, H, Dk).swapaxes(0, 1)
    kpos = jnp.arange(S)

    def one(args):
        i, qi = args
        s = jnp.einsum('bqhd,bkhd->bhqk', qi, k).astype(F32) * scale
        qpos = i * Q_BLOCK + jnp.arange(Q_BLOCK)
        s = jnp.where(kpos[None, :] <= qpos[:, None], s, -jnp.inf)
        p = jax.nn.softmax(s, axis=-1).astype(v.dtype)
        return jnp.einsum('bhqk,bkhd->bqhd', p, v)

    o = lax.map(one, (jnp.arange(nb), qb))
    return o.swapaxes(0, 1).reshape(B, S, H, v.shape[-1])


def _j_mla_mixer(cq, ckv, kpe, pos, q_norm, w_uq, kv_norm, w_ukv):
    B, S, _ = cq.shape
    q = (_j_rms_norm(cq, q_norm) @ w_uq).reshape(B, S, MLA_HEADS, MLA_NOPE + MLA_ROPE)
    kv = (_j_rms_norm(ckv, kv_norm) @ w_ukv).reshape(B, S, MLA_HEADS, MLA_NOPE + MLA_V)
    q = jnp.concatenate([q[..., :MLA_NOPE], _j_rope(q[..., MLA_NOPE:], pos, MLA_ROPE)], axis=-1)
    k_pe = _j_rope(kpe[:, :, None, :], pos, MLA_ROPE)
    k = jnp.concatenate([kv[..., :MLA_NOPE], jnp.broadcast_to(k_pe, (B, S, MLA_HEADS, MLA_ROPE))], axis=-1)
    o = _j_blocked_causal_attention(q, k, kv[..., MLA_NOPE:], (MLA_NOPE + MLA_ROPE) ** -0.5)
    return o.reshape(B, S, MLA_HEADS * MLA_V)


def _j_s5_mixer(u, a_re, a_im, b_re, b_im, c_re, c_im, d_skip, log_step, w_glu, b_glu):
    B, S, CH = u.shape
    NG = CH // S5_GROUP
    uf = u.astype(F32).reshape(B, S, NG, S5_GROUP)
    step = jnp.exp(log_step.astype(F32))[:, None]
    are, aim = a_re.astype(F32), a_im.astype(F32)
    mag = jnp.exp(are * step)
    lb_re, lb_im = mag * jnp.cos(aim * step), mag * jnp.sin(aim * step)
    den = are * are + aim * aim
    nr, ni = lb_re - 1.0, lb_im
    g_re = (nr * are + ni * aim) / den
    g_im = (ni * are - nr * aim) / den
    br, bi = b_re.astype(F32), b_im.astype(F32)
    bb_re = g_re[..., None] * br - g_im[..., None] * bi
    bb_im = g_re[..., None] * bi + g_im[..., None] * br
    bu_re = jnp.einsum('bsgc,gpc->bsgp', uf, bb_re)
    bu_im = jnp.einsum('bsgc,gpc->bsgp', uf, bb_im)
    lam_re = jnp.broadcast_to(lb_re, bu_re.shape)
    lam_im = jnp.broadcast_to(lb_im, bu_re.shape)

    def combine(e1, e2):
        a1r, a1i, b1r, b1i = e1
        a2r, a2i, b2r, b2i = e2
        return (a2r * a1r - a2i * a1i, a2r * a1i + a2i * a1r,
                a2r * b1r - a2i * b1i + b2r, a2r * b1i + a2i * b1r + b2i)

    _, _, xr, xi = lax.associative_scan(combine, (lam_re, lam_im, bu_re, bu_im), axis=1)
    y = jnp.einsum('bsgp,gcp->bsgc', xr, c_re.astype(F32)) - jnp.einsum('bsgp,gcp->bsgc', xi, c_im.astype(F32))
    y = y.reshape(B, S, CH) + d_skip.astype(F32) * u.astype(F32)
    y = jax.nn.gelu(y).astype(u.dtype)
    y = y * jax.nn.sigmoid(y @ w_glu + b_glu)
    return y


def _j_compress_blocks(xb, pos_emb, w1, w2):
    B, N, L, D = xb.shape
    h = jax.nn.gelu((xb + pos_emb).reshape(B, N, L * D) @ w1)
    return h @ w2


def _j_nsa_compressed(q, kc, vc, pos_k, w1k, w2k, pos_v, w1v, w2v, scale):
    B, S, H, D = q.shape
    n_cmp = (S - CMP_BLOCK) // CMP_STRIDE + 1
    idx = np.arange(n_cmp)[:, None] * CMP_STRIDE + np.arange(CMP_BLOCK)[None, :]
    k_cmp = _j_compress_blocks(kc[:, idx], pos_k, w1k, w2k)
    v_cmp = _j_compress_blocks(vc[:, idx], pos_v, w1v, w2v)
    s = jnp.einsum('bshd,bnd->bhsn', q, k_cmp).astype(F32) * scale
    blk_end = np.arange(n_cmp) * CMP_STRIDE + CMP_BLOCK - 1
    valid = jnp.arange(S)[:, None] >= blk_end[None, :]
    p = jax.nn.softmax(jnp.where(valid, s, -1e30), axis=-1)
    p = jnp.where(valid, p, 0.0)
    o = jnp.einsum('bhsn,bnd->bshd', p.astype(v_cmp.dtype), v_cmp)
    return o, p


def _j_nsa_select_blocks(p_cmp):
    B, H, S, n_cmp = p_cmp.shape
    n_sel = S // SEL_BLOCK
    sel_start = np.arange(n_sel) * SEL_BLOCK
    cmp_start = np.arange(n_cmp) * CMP_STRIDE
    overlap = ((cmp_start[:, None] < sel_start[None, :] + SEL_BLOCK)
               & (cmp_start[:, None] + CMP_BLOCK > sel_start[None, :])).astype(np.float32)
    imp = jnp.einsum('bhsn,nj->bsj', p_cmp, jnp.asarray(overlap))
    t_blk = jnp.arange(S) // SEL_BLOCK
    j = jnp.arange(n_sel)
    forced = (j[None, :] == 0) | (j[None, :] == t_blk[:, None]) | (j[None, :] == t_blk[:, None] - 1)
    future = j[None, :] > t_blk[:, None]
    imp = jnp.where(forced, 1e6, jnp.where(future, -1e6, imp))
    _, sel_idx = lax.top_k(imp, min(SEL_TOPK, n_sel))
    return sel_idx


def _j_nsa_selected(q, ks, vs, sel_idx, scale):
    B, S, H, D = q.shape
    n_sel = S // SEL_BLOCK
    nb = S // Q_BLOCK
    n_top = sel_idx.shape[-1]
    ks_b = ks.reshape(B, n_sel, SEL_BLOCK, D)
    vs_b = vs.reshape(B, n_sel, SEL_BLOCK, D)
    qb = q.reshape(B, nb, Q_BLOCK, H, D).swapaxes(0, 1)
    ib = sel_idx.reshape(B, nb, Q_BLOCK, n_top).swapaxes(0, 1)
    bidx = jnp.arange(B)[:, None, None]

    def one(args):
        i, qi, ii = args
        kg = ks_b[bidx, ii]
        vg = vs_b[bidx, ii]
        s = jnp.einsum('bqhd,bqkld->bhqkl', qi, kg).astype(F32) * scale
        kpos = ii[..., None] * SEL_BLOCK + jnp.arange(SEL_BLOCK)
        qpos = i * Q_BLOCK + jnp.arange(Q_BLOCK)
        mask = kpos <= qpos[None, :, None, None]
        s = jnp.where(mask[:, None], s, -jnp.inf)
        p = jax.nn.softmax(s.reshape(B, H, Q_BLOCK, n_top * SEL_BLOCK), axis=-1).reshape(s.shape)
        return jnp.einsum('bhqkl,bqkld->bqhd', p.astype(vg.dtype), vg)

    o = lax.map(one, (jnp.arange(nb), qb, ib))
    return o.swapaxes(0, 1).reshape(B, S, H, D)


def _j_nsa_window(q, kw, vw, scale):
    B, S, H, D = q.shape
    nb = S // Q_BLOCK
    kp = jnp.pad(kw, ((0, 0), (WINDOW, 0), (0, 0)))
    vp = jnp.pad(vw, ((0, 0), (WINDOW, 0), (0, 0)))
    qb = q.reshape(B, nb, Q_BLOCK, H, D).swapaxes(0, 1)

    def one(args):
        i, qi = args
        kb = lax.dynamic_slice_in_dim(kp, i * Q_BLOCK, WINDOW + Q_BLOCK, axis=1)
        vb = lax.dynamic_slice_in_dim(vp, i * Q_BLOCK, WINDOW + Q_BLOCK, axis=1)
        s = jnp.einsum('bqhd,bkd->bhqk', qi, kb).astype(F32) * scale
        kpos = i * Q_BLOCK - WINDOW + jnp.arange(WINDOW + Q_BLOCK)
        qpos = i * Q_BLOCK + jnp.arange(Q_BLOCK)
        rel = qpos[:, None] - kpos[None, :]
        mask = (rel >= 0) & (rel < WINDOW) & (kpos[None, :] >= 0)
        p = jax.nn.softmax(jnp.where(mask, s, -jnp.inf), axis=-1)
        return jnp.einsum('bhqk,bkd->bqhd', p.astype(vb.dtype), vb)

    o = lax.map(one, (jnp.arange(nb), qb))
    return o.swapaxes(0, 1).reshape(B, S, H, D)


def _j_nsa_mixer(nq, kc, vc, ks, vs, kw, vw, ngate, pos, pos_k, w1k, w2k, pos_v, w1v, w2v):
    B, S, _ = nq.shape
    scale = NSA_DIM ** -0.5
    q = _j_rope(nq.reshape(B, S, NSA_HEADS, NSA_DIM), pos, NSA_ROT)

    def rot_k(t):
        return _j_rope(t[:, :, None, :], pos, NSA_ROT)[:, :, 0, :]

    o_cmp, p_cmp = _j_nsa_compressed(q, rot_k(kc), vc, pos_k, w1k, w2k, pos_v, w1v, w2v, scale)
    sel_idx = _j_nsa_select_blocks(p_cmp)
    o_slc = _j_nsa_selected(q, rot_k(ks), vs, sel_idx, scale)
    o_win = _j_nsa_window(q, rot_k(kw), vw, scale)
    gates = jax.nn.sigmoid(ngate).reshape(B, S, 3, NSA_HEADS, 1)
    o = gates[:, :, 0] * o_cmp + gates[:, :, 1] * o_slc + gates[:, :, 2] * o_win
    return o.reshape(B, S, NSA_HEADS * NSA_DIM)


def _j_chunked_gated_delta_rule(q, k, v, g, beta):
    B, S, H, D = q.shape
    C = GDN_CHUNK
    N = S // C

    def chunks(t):
        return t.reshape(B, N, C, H, -1).transpose(1, 0, 3, 2, 4)

    qc, kc, vc = chunks(q), chunks(k), chunks(v)
    gc = jnp.cumsum(g.reshape(B, N, C, H).transpose(1, 0, 3, 2), axis=-1)
    bc = beta.reshape(B, N, C, H).transpose(1, 0, 3, 2)
    incl = jnp.tril(jnp.ones((C, C), dtype=bool))
    strict = jnp.tril(jnp.ones((C, C), dtype=bool), -1)
    diff = gc[..., :, None] - gc[..., None, :]
    decay = jnp.where(incl, jnp.exp(jnp.where(incl, diff, 0.0)), 0.0)
    k_beta = kc * bc[..., None]
    v_beta = vc * bc[..., None]
    lower = jnp.where(strict, jnp.einsum('nbhid,nbhjd->nbhij', k_beta, kc) * decay, 0.0)
    eye = jnp.eye(C, dtype=q.dtype)
    t_inv = lax.linalg.triangular_solve(eye + lower, jnp.broadcast_to(eye, lower.shape),
                                        left_side=True, lower=True)
    u = t_inv @ v_beta
    w = t_inv @ (k_beta * jnp.exp(gc)[..., None])
    intra = jnp.einsum('nbhid,nbhjd->nbhij', qc, kc) * decay
    q_dec = qc * jnp.exp(gc)[..., None]
    g_last = gc[..., -1]
    k_dec = kc * jnp.exp(g_last[..., None] - gc)[..., None]

    def step(state, xs):
        q_i, u_i, w_i, a_i, k_i, gl_i = xs
        v_new = u_i - w_i @ state
        o_i = q_i @ state + a_i @ v_new
        state = state * jnp.exp(gl_i)[..., None, None] + jnp.swapaxes(k_i, -1, -2) @ v_new
        return state, o_i

    state0 = jnp.zeros((B, H, D, D), q.dtype)
    _, o = lax.scan(step, state0, (q_dec, u, w, intra, k_dec, g_last))
    return o.transpose(1, 0, 3, 2, 4).reshape(B, S, H, D)


def _j_gdn_mixer(gq, gk, gv, gz, ga, gb, conv_w, a_log, dt_bias, o_norm):
    B, S, _ = gq.shape
    qkv = jax.nn.silu(_j_causal_dwconv(jnp.concatenate([gq, gk, gv], axis=-1), conv_w))
    q, k, v = [t.reshape(B, S, GDN_HEADS, GDN_DIM).astype(F32) for t in jnp.split(qkv, 3, axis=-1)]
    q = _j_l2_norm(q) * GDN_DIM ** -0.5
    k = _j_l2_norm(k)
    beta = jax.nn.sigmoid(gb.astype(F32))
    g = -jnp.exp(a_log.astype(F32)) * jax.nn.softplus(ga.astype(F32) + dt_bias.astype(F32))
    o = _j_chunked_gated_delta_rule(q, k, v, g, beta)
    o = _j_rms_norm(o, o_norm) * jax.nn.silu(gz.reshape(B, S, GDN_HEADS, GDN_DIM).astype(F32))
    return o.reshape(B, S, GDN_HEADS * GDN_DIM).astype(gq.dtype)


def _in_sizes(gw):
    nsa = NSA_HEADS * NSA_DIM
    gdn = GDN_HEADS * GDN_DIM
    return (MLA_Q_RANK, MLA_KV_RANK, MLA_ROPE, gw,
            nsa, NSA_DIM, NSA_DIM, NSA_DIM, NSA_DIM, NSA_DIM, NSA_DIM, 3 * NSA_HEADS,
            gdn, gdn, gdn, gdn, GDN_HEADS, GDN_HEADS)


def kernel(x, c, positions, w_ada, b_ada, norm_pre_mix, norm_post_mix, norm_pre_ffn, norm_post_ffn, w_in, w_out, gn_mla, gn_s5, gn_nsa, mla_q_norm, mla_w_uq, mla_kv_norm, mla_w_ukv, s5_a_re, s5_a_im, s5_b_re, s5_b_im, s5_c_re, s5_c_im, s5_d, s5_log_step, s5_w_glu, s5_b_glu, nsa_pos_k, nsa_w1_k, nsa_w2_k, nsa_pos_v, nsa_w1_v, nsa_w2_v, gdn_conv_w, gdn_a_log, gdn_dt_bias, gdn_o_norm, ffn_w_in, ffn_conv_w, ffn_w_out):
    B, S, D = x.shape
    depth = w_in.shape[0]
    gw = s5_d.shape[-1]
    in_cols = w_in.shape[-1]
    in_pad = -in_cols % 512
    sizes = _in_sizes(gw)
    splits = np.cumsum(sizes)[:-1].tolist()
    for l in range(depth):
        mods = ada_mods(c, w_ada[l], b_ada[l])
        sh1, sc1, g1, sh2, sc2, g2 = jnp.split(mods, 6, axis=-1)

        w_in_l = jnp.pad(w_in[l], ((0, 0), (0, in_pad))).astype(BF16)
        proj = normmod_matmul(x, norm_pre_mix[l], sc1, sh1, w_in_l, F32, tm=1024, tn=512)[..., :in_cols]
        (cq, ckv, kpe, u_s5, nq, kc, vc, ks, vs, kw, vw, ngate,
         gq, gk, gv, gz, ga, gb) = jnp.split(proj, splits, axis=-1)
        o_a = _j_mla_mixer(cq, ckv, kpe, positions, mla_q_norm[l], mla_w_uq[l], mla_kv_norm[l], mla_w_ukv[l])
        o_b = _j_s5_mixer(u_s5, s5_a_re[l], s5_a_im[l], s5_b_re[l], s5_b_im[l], s5_c_re[l], s5_c_im[l],
                          s5_d[l], s5_log_step[l], s5_w_glu[l], s5_b_glu[l])
        o_c = _j_nsa_mixer(nq, kc, vc, ks, vs, kw, vw, ngate, positions, nsa_pos_k[l], nsa_w1_k[l],
                           nsa_w2_k[l], nsa_pos_v[l], nsa_w1_v[l], nsa_w2_v[l])
        o_d = _j_gdn_mixer(gq, gk, gv, gz, ga, gb, gdn_conv_w[l], gdn_a_log[l], gdn_dt_bias[l], gdn_o_norm[l])
        gn3 = jnp.stack([gn_mla[l], gn_s5[l], gn_nsa[l]])
        x = mix_out(o_a, o_b, o_c, o_d, gn3, w_out[l].astype(BF16), x, g1, norm_post_mix[l], tm=512)

        gv_ffn = normmod_matmul(x, norm_pre_ffn[l], sc2, sh2, ffn_w_in[l].astype(BF16), BF16, tm=1024, tn=512)
        x = ffn_out(gv_ffn, ffn_conv_w[l], ffn_w_out[l].astype(BF16), x, g2, norm_post_ffn[l], tm=512, tf=512)
    return x
```

```python
import functools

import jax
import jax.numpy as jnp
import numpy as np
from jax import lax
from jax.experimental import pallas as pl
from jax.experimental.pallas import tpu as pltpu

F32 = jnp.float32
BF16 = jnp.bfloat16

NORM_EPS = 1e-6
ROPE_THETA = 500000.0

MLA_HEADS, MLA_NOPE, MLA_ROPE, MLA_V, MLA_Q_RANK, MLA_KV_RANK = 4, 128, 64, 128, 384, 256
S5_GROUP, S5_STATE = 16, 64
NSA_HEADS, NSA_DIM = 4, 128
NSA_ROT = NSA_DIM // 4
CMP_BLOCK, CMP_STRIDE, SEL_BLOCK, SEL_TOPK, WINDOW = 32, 16, 64, 16, 512
GDN_HEADS, GDN_DIM, GDN_CONV, GDN_CHUNK = 4, 128, 4, 64
GROUP_W = 512

LANES = 128
SUBLANES = 8
VMEM_LIMIT_BYTES = 56 * 1024 * 1024
HALO_ROWS_BF16 = 16
MASKED = -1e30

PROJ_COLS = 4992
BLK_S5, BLK_NSA_Q, BLK_GDN_Q, BLK_GDN_Z = 0, 1, 2, 5
BLK_MLA = 4
BLK_NSA_KV = 5
BLK_SMALL = 36
GA_LANE, GB_LANE = 12, 16


def _cparams(*sem):
    return pltpu.CompilerParams(dimension_semantics=sem, vmem_limit_bytes=VMEM_LIMIT_BYTES)


def _rms(x, g):
    return x * lax.rsqrt(jnp.mean(x * x, axis=-1, keepdims=True) + NORM_EPS) * g


def _dot_t(a, b):
    return lax.dot_general(a, b, (((1,), (1,)), ((), ())), preferred_element_type=F32)


def _dot_b(a, b):
    return jnp.dot(a.astype(BF16), b.astype(BF16), preferred_element_type=F32)


def permute_w_in(w_in):
    sizes = (MLA_Q_RANK, MLA_KV_RANK, MLA_ROPE, GROUP_W, GROUP_W) + (NSA_DIM,) * 6 + (3 * NSA_HEADS,) \
        + (GROUP_W,) * 4 + (GDN_HEADS, GDN_HEADS)
    (cq, ckv, kpe, u_s5, nq, kc, vc, ks, vs, kw, vw, ngate, gq, gk, gv, gz, ga, gb) = jnp.split(
        w_in, np.cumsum(sizes)[:-1].tolist(), axis=1)
    d = w_in.shape[0]
    used = 6 * GROUP_W + 768 + 768 + 3 * NSA_HEADS + 2 * GDN_HEADS
    out = jnp.concatenate([u_s5, nq, gq, gk, gv, gz, cq, ckv, kpe, jnp.zeros((d, LANES - MLA_ROPE), w_in.dtype),
                           kc, vc, ks, vs, kw, vw, ngate, ga, gb, jnp.zeros((d, PROJ_COLS - used), w_in.dtype)],
                          axis=1)
    assert out.shape[1] == PROJ_COLS
    return out.astype(BF16)


def _ada_kernel(c_ref, w_ref, b_ref, o_ref):
    cond = jax.nn.silu(c_ref[...]).astype(BF16)
    o_ref[...] = jnp.dot(cond, w_ref[...].astype(BF16), preferred_element_type=F32) + b_ref[...]


def ada_mods(c, w_ada, b_ada):
    B, D = c.shape
    N = w_ada.shape[1]
    c_pad = jnp.zeros((SUBLANES, D), F32).at[:B].set(c)
    tn = 1024
    assert N % tn == 0 and B <= SUBLANES
    out = pl.pallas_call(
        _ada_kernel,
        grid=(N // tn,),
        in_specs=[pl.BlockSpec((SUBLANES, D), lambda j: (0, 0)),
                  pl.BlockSpec((D, tn), lambda j: (0, j)),
                  pl.BlockSpec((1, tn), lambda j: (0, j))],
        out_specs=pl.BlockSpec((SUBLANES, tn), lambda j: (0, j)),
        out_shape=jax.ShapeDtypeStruct((SUBLANES, N), F32),
        compiler_params=_cparams("arbitrary"),
        name="ada_mods",
    )(c_pad, w_ada, b_ada.reshape(1, N))
    return out[:B]


def _normmod_matmul_kernel(x_ref, g_ref, sc_ref, sh_ref, w_ref, o_ref, h_ref):
    @pl.when(pl.program_id(2) == 0)
    def _():
        h_ref[...] = (_rms(x_ref[0], g_ref[...]) * (1.0 + sc_ref[0]) + sh_ref[0]).astype(BF16)

    o_ref[0] = jnp.dot(h_ref[...], w_ref[...], preferred_element_type=F32).astype(o_ref.dtype)


def normmod_matmul(x, gain, sc, sh, w_bf16, out_dtype, tm, tn):
    B, S, D = x.shape
    N = w_bf16.shape[1]
    tm = min(tm, S)
    assert S % tm == 0 and N % tn == 0
    return pl.pallas_call(
        _normmod_matmul_kernel,
        grid=(B, S // tm, N // tn),
        in_specs=[pl.BlockSpec((1, tm, D), lambda b, i, j: (b, i, 0)),
                  pl.BlockSpec((1, D), lambda b, i, j: (0, 0)),
                  pl.BlockSpec((1, 1, D), lambda b, i, j: (b, 0, 0)),
                  pl.BlockSpec((1, 1, D), lambda b, i, j: (b, 0, 0)),
                  pl.BlockSpec((D, tn), lambda b, i, j: (0, j))],
        out_specs=pl.BlockSpec((1, tm, tn), lambda b, i, j: (b, i, j)),
        out_shape=jax.ShapeDtypeStruct((B, S, N), out_dtype),
        scratch_shapes=[pltpu.VMEM((tm, D), BF16)],
        compiler_params=_cparams("arbitrary", "arbitrary", "arbitrary"),
        name="normmod_matmul",
    )(x, gain.reshape(1, D), sc.reshape(B, 1, D), sh.reshape(B, 1, D), w_bf16)


def _mix_out_kernel(oa_ref, ob_ref, oc_ref, od_ref, gn_ref, w_ref, x_ref, gate_ref, gamma_ref, o_ref):
    gw = oa_ref.shape[-1]
    parts = (_rms(oa_ref[0], gn_ref[0:1, :]), _rms(ob_ref[0], gn_ref[1:2, :]),
             _rms(oc_ref[0], gn_ref[2:3, :]), od_ref[0])
    y = None
    for p, part in enumerate(parts):
        t = jnp.dot(part.astype(BF16), w_ref[p * gw:(p + 1) * gw, :], preferred_element_type=F32)
        y = t if y is None else y + t
    o_ref[0] = x_ref[0] + gate_ref[0] * _rms(y, gamma_ref[...])


def mix_out(o_a, o_b, o_c, o_d, gn3, w_out_bf16, x, gate, gamma, tm):
    B, S, D = x.shape
    gw = o_a.shape[-1]
    tm = min(tm, S)
    grp = pl.BlockSpec((1, tm, gw), lambda b, i: (b, i, 0))
    return pl.pallas_call(
        _mix_out_kernel,
        grid=(B, S // tm),
        in_specs=[grp, grp, grp, grp,
                  pl.BlockSpec((3, gw), lambda b, i: (0, 0)),
                  pl.BlockSpec((4 * gw, D), lambda b, i: (0, 0)),
                  pl.BlockSpec((1, tm, D), lambda b, i: (b, i, 0)),
                  pl.BlockSpec((1, 1, D), lambda b, i: (b, 0, 0)),
                  pl.BlockSpec((1, D), lambda b, i: (0, 0))],
        out_specs=pl.BlockSpec((1, tm, D), lambda b, i: (b, i, 0)),
        out_shape=jax.ShapeDtypeStruct((B, S, D), F32),
        compiler_params=_cparams("arbitrary", "arbitrary"),
        name="mix_out",
    )(o_a, o_b, o_c, o_d, gn3, w_out_bf16, x, gate.reshape(B, 1, D), gamma.reshape(1, D))


def _ffn_out_kernel(gate_ref, halo_ref, val_ref, cw_ref, w_ref, x_ref, g2_ref, gamma_ref, o_ref, acc_ref):
    i, k = pl.program_id(1), pl.program_id(2)
    g = gate_ref[0].astype(F32)
    halo = jnp.where(i > 0, halo_ref[0].astype(F32), 0.0)
    hl = halo.shape[0]
    row = lax.broadcasted_iota(jnp.int32, g.shape, 0)
    prev1 = jnp.where(row >= 1, pltpu.roll(g, 1, axis=0), halo[hl - 1:hl, :])
    prev2 = jnp.where(row >= 2, pltpu.roll(g, 2, axis=0),
                      jnp.where(row == 1, halo[hl - 1:hl, :], halo[hl - 2:hl - 1, :]))
    conv = cw_ref[0:1, :] * prev2 + cw_ref[1:2, :] * prev1 + cw_ref[2:3, :] * g
    a = (jax.nn.gelu(conv) * val_ref[0].astype(F32)).astype(BF16)
    part = jnp.dot(a, w_ref[...], preferred_element_type=F32)

    @pl.when(k == 0)
    def _():
        acc_ref[...] = part

    @pl.when(k > 0)
    def _():
        acc_ref[...] += part

    @pl.when(k == pl.num_programs(2) - 1)
    def _():
        o_ref[0] = x_ref[0] + g2_ref[0] * _rms(acc_ref[...], gamma_ref[...])


def ffn_out(gv, conv_w, w_out_bf16, x, g2, gamma, tm, tf):
    B, S, D = x.shape
    F = w_out_bf16.shape[0]
    nk = F // tf
    tm = min(tm, S)
    assert F % tf == 0 and S % tm == 0 and tm % HALO_ROWS_BF16 == 0
    rb = tm // HALO_ROWS_BF16
    return pl.pallas_call(
        _ffn_out_kernel,
        grid=(B, S // tm, nk),
        in_specs=[pl.BlockSpec((1, tm, tf), lambda b, i, k: (b, i, k)),
                  pl.BlockSpec((1, HALO_ROWS_BF16, tf), lambda b, i, k: (b, jnp.maximum(i * rb - 1, 0), k)),
                  pl.BlockSpec((1, tm, tf), lambda b, i, k: (b, i, k + nk)),
                  pl.BlockSpec((3, tf), lambda b, i, k: (0, k)),
                  pl.BlockSpec((tf, D), lambda b, i, k: (k, 0)),
                  pl.BlockSpec((1, tm, D), lambda b, i, k: (b, i, 0)),
                  pl.BlockSpec((1, 1, D), lambda b, i, k: (b, 0, 0)),
                  pl.BlockSpec((1, D), lambda b, i, k: (0, 0))],
        out_specs=pl.BlockSpec((1, tm, D), lambda b, i, k: (b, i, 0)),
        out_shape=jax.ShapeDtypeStruct((B, S, D), F32),
        scratch_shapes=[pltpu.VMEM((tm, D), F32)],
        compiler_params=_cparams("arbitrary", "arbitrary", "arbitrary"),
        name="ffn_out",
    )(gv, gv, gv, conv_w, w_out_bf16, x, g2.reshape(B, 1, D), gamma.reshape(1, D))


def _rope_table_kernel(ang_ref, cos_ref, sin_ref, *, half):
    ang = ang_ref[0]
    lane = lax.broadcasted_iota(jnp.int32, ang.shape, 1)
    cos_ref[0] = jnp.where(lane < 2 * half, jnp.cos(ang), 1.0)
    sin_ref[0] = jnp.where(lane < half, -jnp.sin(ang), jnp.where(lane < 2 * half, jnp.sin(ang), 0.0))


def rope_tables(pos, half, tm=512):
    B, S = pos.shape
    inv_freq = ROPE_THETA ** (-jnp.arange(half, dtype=F32) / half)
    row = jnp.concatenate([inv_freq, inv_freq, jnp.zeros((LANES - 2 * half,), F32)])
    ang = pos.astype(F32)[:, :, None] * row
    tm = min(tm, S)
    blk = pl.BlockSpec((1, tm, LANES), lambda b, i: (b, i, 0))
    return pl.pallas_call(
        functools.partial(_rope_table_kernel, half=half),
        grid=(B, S // tm),
        in_specs=[blk], out_specs=[blk, blk],
        out_shape=[jax.ShapeDtypeStruct((B, S, LANES), F32)] * 2,
        compiler_params=_cparams("arbitrary", "arbitrary"),
        name="rope_tables",
    )(ang)


def _rope(x, cos, sin, half):
    lane = lax.broadcasted_iota(jnp.int32, x.shape, 1)
    swapped = jnp.where(lane < half, pltpu.roll(x, LANES - half, axis=1), pltpu.roll(x, half, axis=1))
    return x * cos + swapped * sin


def _online_step(s, mask, v, carry):
    m, l, acc = carry
    h, tq, tk = s.shape
    s = jnp.where(mask[None], s, MASKED)
    m_new = jnp.maximum(m, jnp.max(s, axis=-1, keepdims=True))
    p = jnp.where(mask[None], jnp.exp(s - m_new), 0.0)
    alpha = jnp.exp(m - m_new)
    l = alpha * l + jnp.sum(p, axis=-1, keepdims=True)
    pv = jnp.dot(p.reshape(h * tq, tk).astype(BF16), v, preferred_element_type=F32).reshape(h, tq, -1)
    return m_new, l, alpha * acc + pv


MLA_QK = 2 * LANES


def mla_weights(w_uq, w_ukv):
    H = MLA_HEADS
    wq = w_uq.reshape(MLA_Q_RANK, H, MLA_NOPE + MLA_ROPE)
    wq = jnp.pad(wq, ((0, 0), (0, 0), (0, MLA_QK - MLA_NOPE - MLA_ROPE))).reshape(MLA_Q_RANK, H * MLA_QK)
    return wq.astype(BF16), w_ukv.astype(BF16)


def _mla_prep_kernel(c_ref, cos_ref, sin_ref, qn_ref, kn_ref, wq_ref, wkv_ref, q_ref, k_ref, v_ref):
    H = MLA_HEADS
    c = c_ref[0]
    cos, sin = cos_ref[0], sin_ref[0]
    scale = (MLA_NOPE + MLA_ROPE) ** -0.5
    cq = _rms(c[:, :MLA_Q_RANK], qn_ref[...]).astype(BF16)
    ckv = _rms(c[:, MLA_Q_RANK:MLA_Q_RANK + MLA_KV_RANK], kn_ref[...]).astype(BF16)
    q = jnp.dot(cq, wq_ref[...], preferred_element_type=F32)
    kv = jnp.dot(ckv, wkv_ref[...], preferred_element_type=F32)
    kpe = _rope(c[:, MLA_Q_RANK + MLA_KV_RANK:], cos, sin, MLA_ROPE // 2).astype(BF16)
    for h in range(H):
        b = h * MLA_QK
        q_ref[0, :, b:b + LANES] = (q[:, b:b + LANES] * scale).astype(BF16)
        q_ref[0, :, b + LANES:b + MLA_QK] = (_rope(q[:, b + LANES:b + MLA_QK], cos, sin, MLA_ROPE // 2) * scale).astype(BF16)
        k_ref[0, :, b:b + LANES] = kv[:, b:b + LANES].astype(BF16)
        k_ref[0, :, b + LANES:b + MLA_QK] = kpe
        v_ref[0, :, h * MLA_V:(h + 1) * MLA_V] = kv[:, b + LANES:b + MLA_QK].astype(BF16)


def mla_prep(proj, cos, sin, q_norm, kv_norm, wq, wkv, tm=512):
    B, S, _ = proj.shape
    tm = min(tm, S)
    H = MLA_HEADS
    cw = MLA_Q_RANK + MLA_KV_RANK + LANES
    tab = pl.BlockSpec((1, tm, LANES), lambda b, i: (b, i, 0))
    full = lambda a: pl.BlockSpec(a.shape, lambda b, i: (0,) * a.ndim)
    qn = q_norm.reshape(1, -1)
    kn = kv_norm.reshape(1, -1)
    wide = pl.BlockSpec((1, tm, H * MLA_QK), lambda b, i: (b, i, 0))
    return pl.pallas_call(
        _mla_prep_kernel,
        grid=(B, S // tm),
        in_specs=[pl.BlockSpec((1, tm, cw), lambda b, i: (b, i, BLK_MLA)), tab, tab,
                  full(qn), full(kn), full(wq), full(wkv)],
        out_specs=[wide, wide, pl.BlockSpec((1, tm, H * MLA_V), lambda b, i: (b, i, 0))],
        out_shape=[jax.ShapeDtypeStruct((B, S, H * MLA_QK), BF16), jax.ShapeDtypeStruct((B, S, H * MLA_QK), BF16),
                   jax.ShapeDtypeStruct((B, S, H * MLA_V), BF16)],
        compiler_params=_cparams("arbitrary", "arbitrary"),
        name="mla_prep",
    )(proj, cos, sin, qn, kn, wq, wkv)


def _mla_attn_kernel(q_ref, k_ref, v_ref, o_ref, *, tk):
    tq = q_ref.shape[1]
    q0 = pl.program_id(2) * tq
    q = q_ref[0]
    init = (jnp.full((1, tq, 1), MASKED, F32), jnp.zeros((1, tq, 1), F32), jnp.zeros((1, tq, MLA_V), F32))
    everything = jnp.full((tq, tk), True)

    def body(masked, kt, carry):
        k0 = pl.multiple_of(kt * tk, tk)
        s = _dot_t(q, k_ref[0, pl.ds(k0, tk), :])[None]
        if masked:
            mask = (k0 + lax.broadcasted_iota(jnp.int32, (tq, tk), 1)) <= (q0 + lax.broadcasted_iota(jnp.int32, (tq, tk), 0))
        else:
            mask = everything
        return _online_step(s, mask, v_ref[0, pl.ds(k0, tk), :], carry)

    n_full = q0 // tk
    n_all = (q0 + tq + tk - 1) // tk
    carry = lax.fori_loop(0, n_full, functools.partial(body, False), init)
    _, l, acc = lax.fori_loop(n_full, n_all, functools.partial(body, True), carry)
    o_ref[0] = (acc / l)[0]


def mla_attention(q, k, v, tq=256, tk=512):
    B, S, _ = q.shape
    H = MLA_HEADS
    tq, tk = min(tq, S), min(tk, S)
    return pl.pallas_call(
        functools.partial(_mla_attn_kernel, tk=tk),
        grid=(B, H, S // tq),
        in_specs=[pl.BlockSpec((1, tq, MLA_QK), lambda b, h, i: (b, i, h)),
                  pl.BlockSpec((1, S, MLA_QK), lambda b, h, i: (b, 0, h)),
                  pl.BlockSpec((1, S, MLA_V), lambda b, h, i: (b, 0, h))],
        out_specs=pl.BlockSpec((1, tq, MLA_V), lambda b, h, i: (b, i, h)),
        out_shape=jax.ShapeDtypeStruct((B, S, H * MLA_V), F32),
        compiler_params=_cparams("arbitrary", "arbitrary", "arbitrary"),
        name="mla_attention",
    )(q, k, v)


S5_CHUNK = 16
HI = lax.Precision.HIGHEST


def s5_params(a_re, a_im, b_re, b_im, c_re, c_im, log_step):
    G, P = a_re.shape
    L = S5_CHUNK
    CI = b_re.shape[-1]
    step = jnp.exp(log_step.astype(F32))[:, None]
    are, aim = a_re.astype(F32), a_im.astype(F32)
    mag = jnp.exp(are * step)
    lb_re, lb_im = mag * jnp.cos(aim * step), mag * jnp.sin(aim * step)
    den = are * are + aim * aim
    nr, ni = lb_re - 1.0, lb_im
    g_re = (nr * are + ni * aim) / den
    g_im = (ni * are - nr * aim) / den
    br, bi = b_re.astype(F32), b_im.astype(F32)
    bb_re = g_re[..., None] * br - g_im[..., None] * bi
    bb_im = g_re[..., None] * bi + g_im[..., None] * br
    d = jnp.arange(L + 1, dtype=F32)[:, None, None]
    pmag = jnp.exp(are * step * d)
    pw_re, pw_im = pmag * jnp.cos(aim * step * d), pmag * jnp.sin(aim * step * d)
    m_re = pw_re[..., None] * bb_re - pw_im[..., None] * bb_im
    m_im = pw_re[..., None] * bb_im + pw_im[..., None] * bb_re
    cr, ci_ = c_re.astype(F32), c_im.astype(F32)
    kd = (jnp.einsum('gop,dgpi->dgoi', cr, m_re[:L], precision=HI)
          - jnp.einsum('gop,dgpi->dgoi', ci_, m_im[:L], precision=HI))
    s_idx = np.arange(L)[:, None]
    t_idx = np.arange(L)[None, :]
    dmat = np.clip(t_idx - s_idx, 0, L - 1)
    tk = jnp.where((t_idx >= s_idx)[:, :, None, None, None], kd[dmat], 0.0)
    T = tk.transpose(2, 0, 4, 1, 3).reshape(G, L * CI, L * cr.shape[1])
    p_re = m_re[:L][::-1].transpose(1, 0, 3, 2).reshape(G, L * CI, P)
    p_im = m_im[:L][::-1].transpose(1, 0, 3, 2).reshape(G, L * CI, P)
    q_re = (cr[None] * pw_re[1:, :, None, :] - ci_[None] * pw_im[1:, :, None, :])
    q_im = -(cr[None] * pw_im[1:, :, None, :] + ci_[None] * pw_re[1:, :, None, :])
    q_re = q_re.transpose(1, 3, 0, 2).reshape(G, P, L * cr.shape[1])
    q_im = q_im.transpose(1, 3, 0, 2).reshape(G, P, L * cr.shape[1])

    def bd(m):
        g2 = m.reshape(G // 2, 2, m.shape[1], m.shape[2])
        z = jnp.zeros_like(g2[:, 0])
        top = jnp.concatenate([g2[:, 0], z], axis=2)
        bot = jnp.concatenate([z, g2[:, 1]], axis=2)
        return jnp.concatenate([top, bot], axis=1)

    lam_l = jnp.stack([pw_re[L], pw_im[L]], axis=0).reshape(2, G // 2, 2 * P).transpose(1, 0, 2)
    return dict(T=T.astype(BF16), p_re=bd(p_re).astype(BF16), p_im=bd(p_im).astype(BF16),
                q_re=bd(q_re).astype(BF16), q_im=bd(q_im).astype(BF16), lam=lam_l)


def _s5_kernel(u_ref, t_ref, pre_ref, pim_ref, qre_ref, qim_ref, lam_ref, y_ref, xre_ref, xim_ref):
    rows = u_ref.shape[0]
    nj = rows // SUBLANES
    cw = t_ref.shape[1]
    u = u_ref[...]
    xre_ref[SUBLANES:, :] = jnp.dot(u, pre_ref[0], preferred_element_type=F32)
    xim_ref[SUBLANES:, :] = jnp.dot(u, pim_ref[0], preferred_element_type=F32)
    lam = lam_ref[0]
    lr = jnp.broadcast_to(lam[0:1, :], (SUBLANES, lam.shape[-1]))
    li = jnp.broadcast_to(lam[1:2, :], (SUBLANES, lam.shape[-1]))

    def fwd(j, carry):
        xr, xi = carry
        r0 = pl.multiple_of(SUBLANES + j * SUBLANES, SUBLANES)
        nr = lr * xr - li * xi + xre_ref[pl.ds(r0, SUBLANES), :]
        ni = lr * xi + li * xr + xim_ref[pl.ds(r0, SUBLANES), :]
        xre_ref[pl.ds(r0, SUBLANES), :] = nr
        xim_ref[pl.ds(r0, SUBLANES), :] = ni
        return nr, ni

    zero = jnp.zeros_like(lr)
    lax.fori_loop(0, nj, fwd, (zero, zero))
    row = lax.broadcasted_iota(jnp.int32, lr.shape, 0)
    er = jnp.where(row >= 4, pltpu.roll(xre_ref[rows:rows + SUBLANES, :], 4, axis=0), 0.0)
    ei = jnp.where(row >= 4, pltpu.roll(xim_ref[rows:rows + SUBLANES, :], 4, axis=0), 0.0)
    xre_ref[0:SUBLANES, :] = er
    xim_ref[0:SUBLANES, :] = ei

    def fix(j, carry):
        zr, zi = carry
        r0 = pl.multiple_of(SUBLANES + j * SUBLANES, SUBLANES)
        nr = lr * zr - li * zi
        ni = lr * zi + li * zr
        xre_ref[pl.ds(r0, SUBLANES), :] += nr
        xim_ref[pl.ds(r0, SUBLANES), :] += ni
        return nr, ni

    lax.fori_loop(0, nj, fix, (er, ei))
    y = (jnp.dot(xre_ref[0:rows, :].astype(BF16), qre_ref[0], preferred_element_type=F32)
         + jnp.dot(xim_ref[0:rows, :].astype(BF16), qim_ref[0], preferred_element_type=F32))
    y_ref[:, 0:cw] = y[:, 0:cw] + jnp.dot(u[:, 0:cw], t_ref[0], preferred_element_type=F32)
    y_ref[:, cw:] = y[:, cw:] + jnp.dot(u[:, cw:], t_ref[1], preferred_element_type=F32)


def s5_scan(u, prm):
    B, S, CH = u.shape
    L = S5_CHUNK
    G = CH // S5_GROUP
    assert 2 * B == SUBLANES and S % (2 * L) == 0
    J = S // (2 * L)
    rows = J * SUBLANES
    cw = L * S5_GROUP
    um = u.astype(BF16).reshape(B, 2, J, L, G, S5_GROUP).transpose(2, 1, 0, 4, 3, 5).reshape(rows, G * cw)
    P2 = 2 * S5_STATE
    y = pl.pallas_call(
        _s5_kernel,
        grid=(G // 2,),
        in_specs=[pl.BlockSpec((rows, 2 * cw), lambda k: (0, k)),
                  pl.BlockSpec((2, cw, cw), lambda k: (k, 0, 0)),
                  pl.BlockSpec((1, 2 * cw, P2), lambda k: (k, 0, 0)),
                  pl.BlockSpec((1, 2 * cw, P2), lambda k: (k, 0, 0)),
                  pl.BlockSpec((1, P2, 2 * cw), lambda k: (k, 0, 0)),
                  pl.BlockSpec((1, P2, 2 * cw), lambda k: (k, 0, 0)),
                  pl.BlockSpec((1, 2, P2), lambda k: (k, 0, 0))],
        out_specs=pl.BlockSpec((rows, 2 * cw), lambda k: (0, k)),
        out_shape=jax.ShapeDtypeStruct((rows, G * cw), F32),
        scratch_shapes=[pltpu.VMEM((rows + SUBLANES, P2), F32), pltpu.VMEM((rows + SUBLANES, P2), F32)],
        compiler_params=_cparams("arbitrary"),
        name="s5_scan",
    )(um, prm["T"], prm["p_re"], prm["p_im"], prm["q_re"], prm["q_im"], prm["lam"])
    return y.reshape(J, 2, B, G, L, S5_GROUP).transpose(2, 1, 0, 4, 3, 5).reshape(B, S, CH)


def _s5_post_kernel(y_ref, u_ref, d_ref, w_ref, b_ref, o_ref):
    y = jax.nn.gelu(y_ref[0] + d_ref[...] * u_ref[0])
    z = jnp.dot(y.astype(BF16), w_ref[...], preferred_element_type=F32) + b_ref[...]
    o_ref[0] = y * jax.nn.sigmoid(z)


def s5_post(y, proj, d_skip, w_glu_bf16, b_glu, tm=512):
    B, S, CH = y.shape
    tm = min(tm, S)
    blk = pl.BlockSpec((1, tm, CH), lambda b, i: (b, i, 0))
    vec = pl.BlockSpec((1, CH), lambda b, i: (0, 0))
    return pl.pallas_call(
        _s5_post_kernel,
        grid=(B, S // tm),
        in_specs=[blk, pl.BlockSpec((1, tm, CH), lambda b, i: (b, i, BLK_S5)), vec,
                  pl.BlockSpec((CH, CH), lambda b, i: (0, 0)), vec],
        out_specs=blk,
        out_shape=jax.ShapeDtypeStruct((B, S, CH), F32),
        compiler_params=_cparams("arbitrary", "arbitrary"),
        name="s5_post",
    )(y, proj, d_skip.reshape(1, CH), w_glu_bf16, b_glu.reshape(1, CH))


def _nsa_prep_kernel(kv_ref, cos_ref, sin_ref, kc_ref, vc_ref, ks_ref, vs_ref, kw_ref, vw_ref):
    cos, sin = cos_ref[0], sin_ref[0]
    outs = (kc_ref, vc_ref, ks_ref, vs_ref, kw_ref, vw_ref)
    for n, o_ref in enumerate(outs):
        x = kv_ref[0, :, n * NSA_DIM:(n + 1) * NSA_DIM]
        if n % 2 == 0:
            x = _rope(x, cos, sin, NSA_ROT // 2)
        o_ref[0] = x.astype(BF16)


def nsa_prep(proj, cos, sin, tm=512):
    B, S, _ = proj.shape
    tm = min(tm, S)
    one = pl.BlockSpec((1, tm, NSA_DIM), lambda b, i: (b, i, 0))
    return pl.pallas_call(
        _nsa_prep_kernel,
        grid=(B, S // tm),
        in_specs=[pl.BlockSpec((1, tm, 6 * NSA_DIM), lambda b, i: (b, i, BLK_NSA_KV)), one, one],
        out_specs=[one] * 6,
        out_shape=[jax.ShapeDtypeStruct((B, S, NSA_DIM), BF16)] * 6,
        compiler_params=_cparams("arbitrary", "arbitrary"),
        name="nsa_prep",
    )(proj, cos, sin)


def _nsa_compress_kernel(xk_ref, xv_ref, pk_ref, w1k_ref, w2k_ref, pv_ref, w1v_ref, w2v_ref, ok_ref, ov_ref):
    for x_ref, p_ref, w1_ref, w2_ref, o_ref in ((xk_ref, pk_ref, w1k_ref, w2k_ref, ok_ref),
                                                (xv_ref, pv_ref, w1v_ref, w2v_ref, ov_ref)):
        x = x_ref[0]
        half = x.shape[1]
        a = jnp.dot(x, w1_ref[0:half, :], preferred_element_type=F32)
        b = jnp.dot(x, w1_ref[half:, :], preferred_element_type=F32)
        c = jnp.dot(p_ref[...], w1_ref[...], preferred_element_type=F32)[0:1, :]
        pre = a + pltpu.roll(b, x.shape[0] - 1, axis=0) + c
        h = jax.nn.gelu(pre)
        o_ref[0] = jnp.dot(h.astype(BF16), w2_ref[...], preferred_element_type=F32).astype(BF16)


def nsa_compress(kc, vc, pos_k, w1k, w2k, pos_v, w1v, w2v):
    B, S, D = kc.shape
    nseg = S // CMP_STRIDE
    seg_w = CMP_STRIDE * D
    xk = kc.reshape(B, nseg, seg_w)
    xv = vc.reshape(B, nseg, seg_w)

    def posrow(p):
        return jnp.broadcast_to(p.reshape(1, CMP_BLOCK * D), (SUBLANES, CMP_BLOCK * D)).astype(BF16)

    xs = pl.BlockSpec((1, nseg, seg_w), lambda b: (b, 0, 0))
    ps = pl.BlockSpec((SUBLANES, CMP_BLOCK * D), lambda b: (0, 0))
    w1s = pl.BlockSpec((CMP_BLOCK * D, D), lambda b: (0, 0))
    w2s = pl.BlockSpec((D, D), lambda b: (0, 0))
    os_ = pl.BlockSpec((1, nseg, D), lambda b: (b, 0, 0))
    return pl.pallas_call(
        _nsa_compress_kernel,
        grid=(B,),
        in_specs=[xs, xs, ps, w1s, w2s, ps, w1s, w2s],
        out_specs=[os_, os_],
        out_shape=[jax.ShapeDtypeStruct((B, nseg, D), BF16)] * 2,
        compiler_params=_cparams("arbitrary"),
        name="nsa_compress",
    )(xk, xv, posrow(pos_k), w1k.astype(BF16), w2k.astype(BF16), posrow(pos_v), w1v.astype(BF16), w2v.astype(BF16))


def _nsa_attn_kernel(q_ref, gate_ref, cos_ref, sin_ref, kc_ref, vc_ref, ks_ref, vs_ref, kw_ref, vw_ref, o_ref,
                     *, tk_sel):
    H, D = NSA_HEADS, NSA_DIM
    tq = q_ref.shape[1]
    S = ks_ref.shape[1]
    n_cmp = kc_ref.shape[1]
    n_sel = S // SEL_BLOCK
    q0 = pl.program_id(1) * tq
    scale = D ** -0.5
    cos, sin = cos_ref[0], sin_ref[0]
    q = jnp.concatenate(
        [(_rope(q_ref[0, :, h * D:(h + 1) * D], cos, sin, NSA_ROT // 2) * scale).astype(BF16) for h in range(H)],
        axis=0)

    t_c = q0 + lax.broadcasted_iota(jnp.int32, (tq, n_cmp), 0)
    n_c = lax.broadcasted_iota(jnp.int32, (tq, n_cmp), 1)
    valid = (t_c >= n_c * CMP_STRIDE + (CMP_BLOCK - 1)) & (n_c < n_cmp - 1)
    s = _dot_t(q, kc_ref[0]).reshape(H, tq, n_cmp)
    s = jnp.where(valid[None], s, MASKED)
    e = jnp.exp(s - jnp.max(s, axis=-1, keepdims=True))
    p = jnp.where(valid[None], e / jnp.sum(e, axis=-1, keepdims=True), 0.0)
    o_cmp = jnp.dot(p.reshape(H * tq, n_cmp).astype(BF16), vc_ref[0], preferred_element_type=F32).reshape(H, tq, D)

    psum = jnp.sum(p, axis=0)
    n_o = lax.broadcasted_iota(jnp.int32, (n_cmp, n_sel), 0) * CMP_STRIDE
    j_o = lax.broadcasted_iota(jnp.int32, (n_cmp, n_sel), 1) * SEL_BLOCK
    overlap = jnp.where((n_o < j_o + SEL_BLOCK) & (n_o + CMP_BLOCK > j_o), 1.0, 0.0).astype(BF16)
    p_hi = psum.astype(BF16)
    p_lo = (psum - p_hi.astype(F32)).astype(BF16)
    imp = (jnp.dot(p_hi, overlap, preferred_element_type=F32) + jnp.dot(p_lo, overlap, preferred_element_type=F32))
    t_s = q0 + lax.broadcasted_iota(jnp.int32, (tq, n_sel), 0)
    j_i = lax.broadcasted_iota(jnp.int32, (tq, n_sel), 1)
    t_blk = t_s // SEL_BLOCK
    forced = (j_i == 0) | (j_i == t_blk) | (j_i == t_blk - 1)
    imp = jnp.where(forced, 1e6, jnp.where(j_i > t_blk, -1e6, imp))
    j_s = j_i.astype(F32)
    sel = jnp.zeros((tq, n_sel), F32)
    for _ in range(min(SEL_TOPK, n_sel)):
        mx = jnp.max(imp, axis=-1, keepdims=True)
        first = jnp.min(jnp.where(imp == mx, j_s, float(n_sel)), axis=-1, keepdims=True)
        hit = j_s == first
        sel = jnp.where(hit, 1.0, sel)
        imp = jnp.where(hit, -jnp.inf, imp)
    sel_b = sel.astype(BF16)

    init = (jnp.full((H, tq, 1), MASKED, F32), jnp.zeros((H, tq, 1), F32), jnp.zeros((H, tq, D), F32))

    def sel_body(kt, carry):
        k0 = pl.multiple_of(kt * tk_sel, tk_sel)
        kblk = ks_ref[0, pl.ds(k0, tk_sel), :]
        vblk = vs_ref[0, pl.ds(k0, tk_sel), :]
        sc = _dot_t(q, kblk).reshape(H, tq, tk_sel)
        j_e = lax.broadcasted_iota(jnp.int32, (n_sel, tk_sel), 0)
        c_e = (k0 + lax.broadcasted_iota(jnp.int32, (n_sel, tk_sel), 1)) // SEL_BLOCK
        expand = jnp.where(j_e == c_e, 1.0, 0.0).astype(BF16)
        chosen = jnp.dot(sel_b, expand, preferred_element_type=F32)
        kpos = k0 + lax.broadcasted_iota(jnp.int32, (tq, tk_sel), 1)
        tpos = q0 + lax.broadcasted_iota(jnp.int32, (tq, tk_sel), 0)
        return _online_step(sc, (chosen > 0.5) & (kpos <= tpos), vblk, carry)

    n_kt = (q0 + tq + tk_sel - 1) // tk_sel
    _, l_s, acc_s = lax.fori_loop(0, n_kt, sel_body, init)
    o_slc = acc_s / l_s

    n_w = WINDOW // tq + 1
    first_w = jnp.maximum(n_w - 1 - pl.program_id(1), 0)

    def win_body(w, carry):
        k0 = pl.multiple_of(q0 - WINDOW + w * tq, tq)
        kblk = kw_ref[0, pl.ds(k0, tq), :]
        vblk = vw_ref[0, pl.ds(k0, tq), :]
        sc = _dot_t(q, kblk).reshape(H, tq, tq)
        rel = (q0 + lax.broadcasted_iota(jnp.int32, (tq, tq), 0)) - (k0 + lax.broadcasted_iota(jnp.int32, (tq, tq), 1))
        return _online_step(sc, (rel >= 0) & (rel < WINDOW), vblk, carry)

    _, l_w, acc_w = lax.fori_loop(first_w, n_w, win_body, init)
    o_win = acc_w / l_w

    gates = jax.nn.sigmoid(gate_ref[0])
    for h in range(H):
        o_ref[0, :, h * D:(h + 1) * D] = (gates[:, h:h + 1] * o_cmp[h] + gates[:, H + h:H + h + 1] * o_slc[h]
                                          + gates[:, 2 * H + h:2 * H + h + 1] * o_win[h])


def nsa_attention(proj, cos, sin, kcmp, vcmp, ks, vs, kw, vw, tq=128):
    B, S, _ = proj.shape
    D = NSA_DIM
    n_cmp = kcmp.shape[1]
    assert S % tq == 0 and WINDOW % tq == 0
    tk_sel = min(256, S)
    tile = pl.BlockSpec((1, tq, D), lambda b, i: (b, i, 0))
    seq = pl.BlockSpec((1, S, D), lambda b, i: (b, 0, 0))
    cmp_ = pl.BlockSpec((1, n_cmp, D), lambda b, i: (b, 0, 0))
    return pl.pallas_call(
        functools.partial(_nsa_attn_kernel, tk_sel=tk_sel),
        grid=(B, S // tq),
        in_specs=[pl.BlockSpec((1, tq, NSA_HEADS * D), lambda b, i: (b, i, BLK_NSA_Q)),
                  pl.BlockSpec((1, tq, LANES), lambda b, i: (b, i, BLK_SMALL)),
                  tile, tile, cmp_, cmp_, seq, seq, seq, seq],
        out_specs=pl.BlockSpec((1, tq, NSA_HEADS * D), lambda b, i: (b, i, 0)),
        out_shape=jax.ShapeDtypeStruct((B, S, NSA_HEADS * D), F32),
        compiler_params=_cparams("arbitrary", "arbitrary"),
        name="nsa_attention",
    )(proj, proj, cos, sin, kcmp, vcmp, ks, vs, kw, vw)


GDN_W = GDN_HEADS * GDN_DIM


def _causal_conv_silu(x, halo, w_ref, c0):
    taps = w_ref.shape[0]
    row = lax.broadcasted_iota(jnp.int32, x.shape, 0)
    acc = x * w_ref[taps - 1:taps, c0:c0 + x.shape[1]]
    for d in range(1, taps):
        shifted = pltpu.roll(x, d, axis=0)
        for r in range(d):
            shifted = jnp.where(row == r, halo[SUBLANES - d + r:SUBLANES - d + r + 1, :], shifted)
        acc = acc + shifted * w_ref[taps - 1 - d:taps - d, c0:c0 + x.shape[1]]
    return acc * jax.nn.sigmoid(acc)


def _gdn_prep_kernel(q_ref, k_ref, v_ref, hq_ref, hk_ref, hv_ref, sm_ref, w_ref, a_ref, dt_ref,
                     qo_ref, ko_ref, vo_ref, go_ref):
    first = pl.program_id(1) == 0
    for n, (x_ref, h_ref, o_ref) in enumerate(((q_ref, hq_ref, qo_ref), (k_ref, hk_ref, ko_ref), (v_ref, hv_ref, vo_ref))):
        halo = jnp.where(first, 0.0, h_ref[0])
        y = _causal_conv_silu(x_ref[0], halo, w_ref, n * GDN_W)
        for h in range(GDN_HEADS):
            yh = y[:, h * GDN_DIM:(h + 1) * GDN_DIM]
            if n < 2:
                yh = yh * lax.rsqrt(jnp.sum(yh * yh, axis=-1, keepdims=True) + 1e-6)
            if n == 0:
                yh = yh * GDN_DIM ** -0.5
            o_ref[0, :, h * GDN_DIM:(h + 1) * GDN_DIM] = yh
    sm = sm_ref[0]
    lane = lax.broadcasted_iota(jnp.int32, sm.shape, 1)
    g = -a_ref[...] * jax.nn.softplus(sm + dt_ref[...])
    go_ref[0] = jnp.where((lane >= GA_LANE) & (lane < GA_LANE + GDN_HEADS), g, jax.nn.sigmoid(sm))


def gdn_prep(proj, conv_w, a_log, dt_bias, tm=256):
    B, S, _ = proj.shape
    tm = min(tm, S)
    rb = tm // SUBLANES
    a_row = jnp.zeros((1, LANES), F32).at[0, GA_LANE:GA_LANE + GDN_HEADS].set(jnp.exp(a_log.astype(F32)))
    dt_row = jnp.zeros((1, LANES), F32).at[0, GA_LANE:GA_LANE + GDN_HEADS].set(dt_bias.astype(F32))

    def main(n):
        return pl.BlockSpec((1, tm, GDN_W), lambda b, i: (b, i, BLK_GDN_Q + n))

    def halo(n):
        return pl.BlockSpec((1, SUBLANES, GDN_W), lambda b, i: (b, jnp.maximum(i * rb - 1, 0), BLK_GDN_Q + n))

    out = pl.BlockSpec((1, tm, GDN_W), lambda b, i: (b, i, 0))
    sm = pl.BlockSpec((1, tm, LANES), lambda b, i: (b, i, BLK_SMALL))
    row = pl.BlockSpec((1, LANES), lambda b, i: (0, 0))
    return pl.pallas_call(
        _gdn_prep_kernel,
        grid=(B, S // tm),
        in_specs=[main(0), main(1), main(2), halo(0), halo(1), halo(2), sm,
                  pl.BlockSpec(conv_w.shape, lambda b, i: (0, 0)), row, row],
        out_specs=[out, out, out, pl.BlockSpec((1, tm, LANES), lambda b, i: (b, i, 0))],
        out_shape=[jax.ShapeDtypeStruct((B, S, GDN_W), F32)] * 3 + [jax.ShapeDtypeStruct((B, S, LANES), F32)],
        compiler_params=_cparams("arbitrary", "arbitrary"),
        name="gdn_prep",
    )(proj, proj, proj, proj, proj, proj, proj, conv_w, a_row, dt_row)


def _split3(x):
    hi = x.astype(BF16)
    r = x - hi.astype(F32)
    mid = r.astype(BF16)
    return hi, mid, (r - mid.astype(F32)).astype(BF16)


def _gdn_core_kernel(q_ref, k_ref, v_ref, g_ref, grow_ref, z_ref, on_ref, o_ref, st_ref):
    C, D, H = GDN_CHUNK, GDN_DIM, GDN_HEADS
    tc = q_ref.shape[1]

    @pl.when(pl.program_id(1) == 0)
    def _():
        st_ref[...] = jnp.zeros_like(st_ref)

    ri = lax.broadcasted_iota(jnp.int32, (C, C), 0)
    ci = lax.broadcasted_iota(jnp.int32, (C, C), 1)
    incl = ri >= ci
    strict = ri > ci
    lower_ones = jnp.where(incl, 1.0, 0.0).astype(BF16)
    upper_ones = jnp.where(ri <= ci, 1.0, 0.0).astype(BF16)
    eye = jnp.where(ri == ci, 1.0, 0.0)
    pair_masks = []
    s_ = 1
    while s_ < C:
        pair_masks.append((ri // (2 * s_) == ci // (2 * s_)) & (ri % (2 * s_) >= s_) & (ci % (2 * s_) < s_))
        s_ *= 2

    def chunk(c, _):
        r0 = pl.multiple_of(c * C, C)
        gates = g_ref[0, pl.ds(r0, C), :]
        gc_all = sum(jnp.dot(lower_ones, p, preferred_element_type=F32) for p in _split3(gates))
        gr_all = sum(jnp.dot(p, upper_ones, preferred_element_type=F32) for p in _split3(grow_ref[0, c]))
        for h in range(H):
            sl = slice(h * D, (h + 1) * D)
            qh = q_ref[0, pl.ds(r0, C), sl]
            kh = k_ref[0, pl.ds(r0, C), sl]
            vh = v_ref[0, pl.ds(r0, C), sl]
            beta = gates[:, GB_LANE + h:GB_LANE + h + 1]
            gc = gc_all[:, GA_LANE + h:GA_LANE + h + 1]
            gr = gr_all[h:h + 1, :]
            decay = jnp.where(incl, jnp.exp(jnp.where(incl, gc - gr, 0.0)), 0.0)
            kb = kh * beta
            egc = jnp.exp(gc)
            low = jnp.where(strict, _dot_t(kb.astype(BF16), kh.astype(BF16)) * decay, 0.0)
            inv = eye - jnp.where(pair_masks[0], low, 0.0)
            for pm in pair_masks[1:]:
                inv = inv - _dot_b(inv, _dot_b(jnp.where(pm, low, 0.0), inv))
            x = _dot_b(inv, jnp.concatenate([vh * beta, kb * egc], axis=1))
            u, w = x[:, :D], x[:, D:]
            intra = jnp.where(incl, _dot_t(qh.astype(BF16), kh.astype(BF16)) * decay, 0.0)
            g_last = gc[C - 1:C, :]
            st = st_ref[h]
            v_new = u - _dot_b(w, st)
            o = _dot_b(qh * egc, st) + _dot_b(intra, v_new)
            k_dec = kh * jnp.exp(g_last - gc)
            st_ref[h] = st * jnp.exp(g_last) + lax.dot_general(
                k_dec.astype(BF16), v_new.astype(BF16), (((0,), (0,)), ((), ())), preferred_element_type=F32)
            zh = z_ref[0, pl.ds(r0, C), sl]
            o_ref[0, pl.ds(r0, C), sl] = _rms(o, on_ref[...]) * (zh * jax.nn.sigmoid(zh))
        return 0

    lax.fori_loop(0, tc // C, chunk, 0)


def gdn_core(q, k, v, gates, proj, o_norm, tc=256):
    B, S, _ = q.shape
    C = GDN_CHUNK
    tc = min(tc, S)
    g_rows = gates[..., GA_LANE:GA_LANE + GDN_HEADS].reshape(B, S // C, C, GDN_HEADS).transpose(0, 1, 3, 2)
    g_rows = jnp.pad(g_rows, ((0, 0), (0, 0), (0, SUBLANES - GDN_HEADS), (0, 0)))
    blk = pl.BlockSpec((1, tc, GDN_W), lambda b, i: (b, i, 0))
    return pl.pallas_call(
        _gdn_core_kernel,
        grid=(B, S // tc),
        in_specs=[blk, blk, blk,
                  pl.BlockSpec((1, tc, LANES), lambda b, i: (b, i, 0)),
                  pl.BlockSpec((1, tc // C, SUBLANES, C), lambda b, i: (b, i, 0, 0)),
                  pl.BlockSpec((1, tc, GDN_W), lambda b, i: (b, i, BLK_GDN_Z)),
                  pl.BlockSpec((1, GDN_DIM), lambda b, i: (0, 0))],
        out_specs=blk,
        out_shape=jax.ShapeDtypeStruct((B, S, GDN_W), F32),
        scratch_shapes=[pltpu.VMEM((GDN_HEADS, GDN_DIM, GDN_DIM), F32)],
        compiler_params=_cparams("arbitrary", "arbitrary"),
        name="gdn_core",
    )(q, k, v, gates, g_rows, proj, o_norm.reshape(1, GDN_DIM))


def kernel(x, c, positions, w_ada, b_ada, norm_pre_mix, norm_post_mix, norm_pre_ffn, norm_post_ffn, w_in, w_out, gn_mla, gn_s5, gn_nsa, mla_q_norm, mla_w_uq, mla_kv_norm, mla_w_ukv, s5_a_re, s5_a_im, s5_b_re, s5_b_im, s5_c_re, s5_c_im, s5_d, s5_log_step, s5_w_glu, s5_b_glu, nsa_pos_k, nsa_w1_k, nsa_w2_k, nsa_pos_v, nsa_w1_v, nsa_w2_v, gdn_conv_w, gdn_a_log, gdn_dt_bias, gdn_o_norm, ffn_w_in, ffn_conv_w, ffn_w_out):
    depth = w_in.shape[0]
    cos_mla, sin_mla = rope_tables(positions, MLA_ROPE // 2)
    cos_nsa, sin_nsa = rope_tables(positions, NSA_ROT // 2)
    for l in range(depth):
        mods = ada_mods(c, w_ada[l], b_ada[l])
        sh1, sc1, g1, sh2, sc2, g2 = jnp.split(mods, 6, axis=-1)

        proj = normmod_matmul(x, norm_pre_mix[l], sc1, sh1, permute_w_in(w_in[l]), F32, tm=512, tn=PROJ_COLS // 3)

        wq, wkv = mla_weights(mla_w_uq[l], mla_w_ukv[l])
        q_a, k_a, v_a = mla_prep(proj, cos_mla, sin_mla, mla_q_norm[l], mla_kv_norm[l], wq, wkv)
        o_a = mla_attention(q_a, k_a, v_a)

        s5p = s5_params(s5_a_re[l], s5_a_im[l], s5_b_re[l], s5_b_im[l], s5_c_re[l], s5_c_im[l], s5_log_step[l])
        y_b = s5_scan(proj[..., BLK_S5 * GROUP_W:(BLK_S5 + 1) * GROUP_W], s5p)
        o_b = s5_post(y_b, proj, s5_d[l], s5_w_glu[l].astype(BF16), s5_b_glu[l])

        kc, vc, ks, vs, kw, vw = nsa_prep(proj, cos_nsa, sin_nsa)
        kcmp, vcmp = nsa_compress(kc, vc, nsa_pos_k[l], nsa_w1_k[l], nsa_w2_k[l], nsa_pos_v[l], nsa_w1_v[l], nsa_w2_v[l])
        o_c = nsa_attention(proj, cos_nsa, sin_nsa, kcmp, vcmp, ks, vs, kw, vw)

        q_d, k_d, v_d, gates_d = gdn_prep(proj, gdn_conv_w[l], gdn_a_log[l], gdn_dt_bias[l])
        o_d = gdn_core(q_d, k_d, v_d, gates_d, proj, gdn_o_norm[l])

        gn3 = jnp.stack([gn_mla[l], gn_s5[l], gn_nsa[l]])
        x = mix_out(o_a, o_b, o_c, o_d, gn3, w_out[l].astype(BF16), x, g1, norm_post_mix[l], tm=512)

        gv_ffn = normmod_matmul(x, norm_pre_ffn[l], sc2, sh2, ffn_w_in[l].astype(BF16), BF16, tm=1024, tn=512)
        x = ffn_out(gv_ffn, ffn_conv_w[l], ffn_w_out[l].astype(BF16), x, g2, norm_post_ffn[l], tm=512, tf=512)
    return x
```

```python
import functools

import jax
import jax.numpy as jnp
import numpy as np
from jax import lax
from jax.experimental import pallas as pl
from jax.experimental.pallas import tpu as pltpu

F32 = jnp.float32
BF16 = jnp.bfloat16

NORM_EPS = 1e-6
ROPE_THETA = 500000.0

MLA_HEADS, MLA_NOPE, MLA_ROPE, MLA_V, MLA_Q_RANK, MLA_KV_RANK = 4, 128, 64, 128, 384, 256
S5_GROUP, S5_STATE = 16, 64
NSA_HEADS, NSA_DIM = 4, 128
NSA_ROT = NSA_DIM // 4
CMP_BLOCK, CMP_STRIDE, SEL_BLOCK, SEL_TOPK, WINDOW = 32, 16, 64, 16, 512
GDN_HEADS, GDN_DIM, GDN_CONV, GDN_CHUNK = 4, 128, 4, 64
GROUP_W = 512

LANES = 128
SUBLANES = 8
VMEM_LIMIT_BYTES = 56 * 1024 * 1024
HALO_ROWS_BF16 = 16
MASKED = -1e30

PROJ_COLS = 4992
BLK_S5, BLK_NSA_Q, BLK_GDN_Q, BLK_GDN_Z = 0, 1, 2, 5
BLK_MLA = 4
BLK_NSA_KV = 5
BLK_SMALL = 36
GA_LANE, GB_LANE = 12, 16


def _cparams(*sem):
    return pltpu.CompilerParams(dimension_semantics=sem, vmem_limit_bytes=VMEM_LIMIT_BYTES)


def _rms(x, g):
    return x * lax.rsqrt(jnp.mean(x * x, axis=-1, keepdims=True) + NORM_EPS) * g


def _dot_t(a, b):
    return lax.dot_general(a, b, (((1,), (1,)), ((), ())), preferred_element_type=F32)


def _dot_b(a, b):
    return jnp.dot(a.astype(BF16), b.astype(BF16), preferred_element_type=F32)


def permute_w_in(w_in):
    sizes = (MLA_Q_RANK, MLA_KV_RANK, MLA_ROPE, GROUP_W, GROUP_W) + (NSA_DIM,) * 6 + (3 * NSA_HEADS,) \
        + (GROUP_W,) * 4 + (GDN_HEADS, GDN_HEADS)
    (cq, ckv, kpe, u_s5, nq, kc, vc, ks, vs, kw, vw, ngate, gq, gk, gv, gz, ga, gb) = jnp.split(
        w_in, np.cumsum(sizes)[:-1].tolist(), axis=1)
    d = w_in.shape[0]
    used = 6 * GROUP_W + 768 + 768 + 3 * NSA_HEADS + 2 * GDN_HEADS
    out = jnp.concatenate([u_s5, nq, gq, gk, gv, gz, cq, ckv, kpe, jnp.zeros((d, LANES - MLA_ROPE), w_in.dtype),
                           kc, vc, ks, vs, kw, vw, ngate, ga, gb, jnp.zeros((d, PROJ_COLS - used), w_in.dtype)],
                          axis=1)
    assert out.shape[1] == PROJ_COLS
    return out.astype(BF16)


def _ada_kernel(c_ref, w_ref, b_ref, o_ref):
    cond = jax.nn.silu(c_ref[...]).astype(BF16)
    o_ref[...] = jnp.dot(cond, w_ref[...].astype(BF16), preferred_element_type=F32) + b_ref[...]


def ada_mods(c, w_ada, b_ada):
    B, D = c.shape
    N = w_ada.shape[1]
    c_pad = jnp.zeros((SUBLANES, D), F32).at[:B].set(c)
    tn = 1024
    assert N % tn == 0 and B <= SUBLANES
    out = pl.pallas_call(
        _ada_kernel,
        grid=(N // tn,),
        in_specs=[pl.BlockSpec((SUBLANES, D), lambda j: (0, 0)),
                  pl.BlockSpec((D, tn), lambda j: (0, j)),
                  pl.BlockSpec((1, tn), lambda j: (0, j))],
        out_specs=pl.BlockSpec((SUBLANES, tn), lambda j: (0, j)),
        out_shape=jax.ShapeDtypeStruct((SUBLANES, N), F32),
        compiler_params=_cparams("arbitrary"),
        name="ada_mods",
    )(c_pad, w_ada, b_ada.reshape(1, N))
    return out[:B]


def _normmod_matmul_kernel(x_ref, g_ref, sc_ref, sh_ref, w_ref, o_ref, h_ref):
    @pl.when(pl.program_id(2) == 0)
    def _():
        h_ref[...] = (_rms(x_ref[0], g_ref[...]) * (1.0 + sc_ref[0]) + sh_ref[0]).astype(BF16)

    o_ref[0] = jnp.dot(h_ref[...], w_ref[...], preferred_element_type=F32).astype(o_ref.dtype)


def normmod_matmul(x, gain, sc, sh, w_bf16, out_dtype, tm, tn):
    B, S, D = x.shape
    N = w_bf16.shape[1]
    tm = min(tm, S)
    assert S % tm == 0 and N % tn == 0
    return pl.pallas_call(
        _normmod_matmul_kernel,
        grid=(B, S // tm, N // tn),
        in_specs=[pl.BlockSpec((1, tm, D), lambda b, i, j: (b, i, 0)),
                  pl.BlockSpec((1, D), lambda b, i, j: (0, 0)),
                  pl.BlockSpec((1, 1, D), lambda b, i, j: (b, 0, 0)),
                  pl.BlockSpec((1, 1, D), lambda b, i, j: (b, 0, 0)),
                  pl.BlockSpec((D, tn), lambda b, i, j: (0, j))],
        out_specs=pl.BlockSpec((1, tm, tn), lambda b, i, j: (b, i, j)),
        out_shape=jax.ShapeDtypeStruct((B, S, N), out_dtype),
        scratch_shapes=[pltpu.VMEM((tm, D), BF16)],
        compiler_params=_cparams("arbitrary", "arbitrary", "arbitrary"),
        name="normmod_matmul",
    )(x, gain.reshape(1, D), sc.reshape(B, 1, D), sh.reshape(B, 1, D), w_bf16)


def _mix_out_kernel(oa_ref, ob_ref, oc_ref, od_ref, gn_ref, w_ref, x_ref, gate_ref, gamma_ref, o_ref):
    gw = oa_ref.shape[-1]
    parts = (_rms(oa_ref[0], gn_ref[0:1, :]), _rms(ob_ref[0], gn_ref[1:2, :]),
             _rms(oc_ref[0], gn_ref[2:3, :]), od_ref[0])
    y = None
    for p, part in enumerate(parts):
        t = jnp.dot(part.astype(BF16), w_ref[p * gw:(p + 1) * gw, :], preferred_element_type=F32)
        y = t if y is None else y + t
    o_ref[0] = x_ref[0] + gate_ref[0] * _rms(y, gamma_ref[...])


def mix_out(o_a, o_b, o_c, o_d, gn3, w_out_bf16, x, gate, gamma, tm):
    B, S, D = x.shape
    gw = o_a.shape[-1]
    tm = min(tm, S)
    grp = pl.BlockSpec((1, tm, gw), lambda b, i: (b, i, 0))
    return pl.pallas_call(
        _mix_out_kernel,
        grid=(B, S // tm),
        in_specs=[grp, grp, grp, grp,
                  pl.BlockSpec((3, gw), lambda b, i: (0, 0)),
                  pl.BlockSpec((4 * gw, D), lambda b, i: (0, 0)),
                  pl.BlockSpec((1, tm, D), lambda b, i: (b, i, 0)),
                  pl.BlockSpec((1, 1, D), lambda b, i: (b, 0, 0)),
                  pl.BlockSpec((1, D), lambda b, i: (0, 0))],
        out_specs=pl.BlockSpec((1, tm, D), lambda b, i: (b, i, 0)),
        out_shape=jax.ShapeDtypeStruct((B, S, D), F32),
        compiler_params=_cparams("arbitrary", "arbitrary"),
        name="mix_out",
    )(o_a, o_b, o_c, o_d, gn3, w_out_bf16, x, gate.reshape(B, 1, D), gamma.reshape(1, D))


def _ffn_out_kernel(gate_ref, halo_ref, val_ref, cw_ref, w_ref, x_ref, g2_ref, gamma_ref, o_ref, acc_ref):
    i, k = pl.program_id(1), pl.program_id(2)
    g = gate_ref[0].astype(F32)
    halo = jnp.where(i > 0, halo_ref[0].astype(F32), 0.0)
    hl = halo.shape[0]
    row = lax.broadcasted_iota(jnp.int32, g.shape, 0)
    prev1 = jnp.where(row >= 1, pltpu.roll(g, 1, axis=0), halo[hl - 1:hl, :])
    prev2 = jnp.where(row >= 2, pltpu.roll(g, 2, axis=0),
                      jnp.where(row == 1, halo[hl - 1:hl, :], halo[hl - 2:hl - 1, :]))
    conv = cw_ref[0:1, :] * prev2 + cw_ref[1:2, :] * prev1 + cw_ref[2:3, :] * g
    a = (jax.nn.gelu(conv) * val_ref[0].astype(F32)).astype(BF16)
    part = jnp.dot(a, w_ref[...], preferred_element_type=F32)

    @pl.when(k == 0)
    def _():
        acc_ref[...] = part

    @pl.when(k > 0)
    def _():
        acc_ref[...] += part

    @pl.when(k == pl.num_programs(2) - 1)
    def _():
        o_ref[0] = x_ref[0] + g2_ref[0] * _rms(acc_ref[...], gamma_ref[...])


def ffn_out(gv, conv_w, w_out_bf16, x, g2, gamma, tm, tf):
    B, S, D = x.shape
    F = w_out_bf16.shape[0]
    nk = F // tf
    tm = min(tm, S)
    assert F % tf == 0 and S % tm == 0 and tm % HALO_ROWS_BF16 == 0
    rb = tm // HALO_ROWS_BF16
    return pl.pallas_call(
        _ffn_out_kernel,
        grid=(B, S // tm, nk),
        in_specs=[pl.BlockSpec((1, tm, tf), lambda b, i, k: (b, i, k)),
                  pl.BlockSpec((1, HALO_ROWS_BF16, tf), lambda b, i, k: (b, jnp.maximum(i * rb - 1, 0), k)),
                  pl.BlockSpec((1, tm, tf), lambda b, i, k: (b, i, k + nk)),
                  pl.BlockSpec((3, tf), lambda b, i, k: (0, k)),
                  pl.BlockSpec((tf, D), lambda b, i, k: (k, 0)),
                  pl.BlockSpec((1, tm, D), lambda b, i, k: (b, i, 0)),
                  pl.BlockSpec((1, 1, D), lambda b, i, k: (b, 0, 0)),
                  pl.BlockSpec((1, D), lambda b, i, k: (0, 0))],
        out_specs=pl.BlockSpec((1, tm, D), lambda b, i, k: (b, i, 0)),
        out_shape=jax.ShapeDtypeStruct((B, S, D), F32),
        scratch_shapes=[pltpu.VMEM((tm, D), F32)],
        compiler_params=_cparams("arbitrary", "arbitrary", "arbitrary"),
        name="ffn_out",
    )(gv, gv, gv, conv_w, w_out_bf16, x, g2.reshape(B, 1, D), gamma.reshape(1, D))


def _rope_table_kernel(ang_ref, cos_ref, sin_ref, *, half):
    ang = ang_ref[0]
    lane = lax.broadcasted_iota(jnp.int32, ang.shape, 1)
    cos_ref[0] = jnp.where(lane < 2 * half, jnp.cos(ang), 1.0)
    sin_ref[0] = jnp.where(lane < half, -jnp.sin(ang), jnp.where(lane < 2 * half, jnp.sin(ang), 0.0))


def rope_tables(pos, half, tm=512):
    B, S = pos.shape
    inv_freq = ROPE_THETA ** (-jnp.arange(half, dtype=F32) / half)
    row = jnp.concatenate([inv_freq, inv_freq, jnp.zeros((LANES - 2 * half,), F32)])
    ang = pos.astype(F32)[:, :, None] * row
    tm = min(tm, S)
    blk = pl.BlockSpec((1, tm, LANES), lambda b, i: (b, i, 0))
    return pl.pallas_call(
        functools.partial(_rope_table_kernel, half=half),
        grid=(B, S // tm),
        in_specs=[blk], out_specs=[blk, blk],
        out_shape=[jax.ShapeDtypeStruct((B, S, LANES), F32)] * 2,
        compiler_params=_cparams("arbitrary", "arbitrary"),
        name="rope_tables",
    )(ang)


def _rope(x, cos, sin, half):
    lane = lax.broadcasted_iota(jnp.int32, x.shape, 1)
    swapped = jnp.where(lane < half, pltpu.roll(x, LANES - half, axis=1), pltpu.roll(x, half, axis=1))
    return x * cos + swapped * sin


def _online_step(s, mask, v, carry):
    m, l, acc = carry
    h, tq, tk = s.shape
    s = jnp.where(mask[None], s, MASKED)
    m_new = jnp.maximum(m, jnp.max(s, axis=-1, keepdims=True))
    p = jnp.where(mask[None], jnp.exp(s - m_new), 0.0)
    alpha = jnp.exp(m - m_new)
    l = alpha * l + jnp.sum(p, axis=-1, keepdims=True)
    pv = jnp.dot(p.reshape(h * tq, tk).astype(BF16), v, preferred_element_type=F32).reshape(h, tq, -1)
    return m_new, l, alpha * acc + pv


MLA_QK = 2 * LANES


def mla_weights(w_uq, w_ukv):
    H = MLA_HEADS
    wq = w_uq.reshape(MLA_Q_RANK, H, MLA_NOPE + MLA_ROPE)
    wq = jnp.pad(wq, ((0, 0), (0, 0), (0, MLA_QK - MLA_NOPE - MLA_ROPE))).reshape(MLA_Q_RANK, H * MLA_QK)
    return wq.astype(BF16), w_ukv.astype(BF16)


def _mla_prep_kernel(c_ref, cos_ref, sin_ref, qn_ref, kn_ref, wq_ref, wkv_ref, q_ref, k_ref, v_ref):
    H = MLA_HEADS
    c = c_ref[0]
    cos, sin = cos_ref[0], sin_ref[0]
    scale = (MLA_NOPE + MLA_ROPE) ** -0.5
    cq = _rms(c[:, :MLA_Q_RANK], qn_ref[...]).astype(BF16)
    ckv = _rms(c[:, MLA_Q_RANK:MLA_Q_RANK + MLA_KV_RANK], kn_ref[...]).astype(BF16)
    q = jnp.dot(cq, wq_ref[...], preferred_element_type=F32)
    kv = jnp.dot(ckv, wkv_ref[...], preferred_element_type=F32)
    kpe = _rope(c[:, MLA_Q_RANK + MLA_KV_RANK:], cos, sin, MLA_ROPE // 2).astype(BF16)
    for h in range(H):
        b = h * MLA_QK
        q_ref[0, :, b:b + LANES] = (q[:, b:b + LANES] * scale).astype(BF16)
        q_ref[0, :, b + LANES:b + MLA_QK] = (_rope(q[:, b + LANES:b + MLA_QK], cos, sin, MLA_ROPE // 2) * scale).astype(BF16)
        k_ref[0, :, b:b + LANES] = kv[:, b:b + LANES].astype(BF16)
        k_ref[0, :, b + LANES:b + MLA_QK] = kpe
        v_ref[0, :, h * MLA_V:(h + 1) * MLA_V] = kv[:, b + LANES:b + MLA_QK].astype(BF16)


def mla_prep(proj, cos, sin, q_norm, kv_norm, wq, wkv, tm=512):
    B, S, _ = proj.shape
    tm = min(tm, S)
    H = MLA_HEADS
    cw = MLA_Q_RANK + MLA_KV_RANK + LANES
    tab = pl.BlockSpec((1, tm, LANES), lambda b, i: (b, i, 0))
    full = lambda a: pl.BlockSpec(a.shape, lambda b, i: (0,) * a.ndim)
    qn = q_norm.reshape(1, -1)
    kn = kv_norm.reshape(1, -1)
    wide = pl.BlockSpec((1, tm, H * MLA_QK), lambda b, i: (b, i, 0))
    return pl.pallas_call(
        _mla_prep_kernel,
        grid=(B, S // tm),
        in_specs=[pl.BlockSpec((1, tm, cw), lambda b, i: (b, i, BLK_MLA)), tab, tab,
                  full(qn), full(kn), full(wq), full(wkv)],
        out_specs=[wide, wide, pl.BlockSpec((1, tm, H * MLA_V), lambda b, i: (b, i, 0))],
        out_shape=[jax.ShapeDtypeStruct((B, S, H * MLA_QK), BF16), jax.ShapeDtypeStruct((B, S, H * MLA_QK), BF16),
                   jax.ShapeDtypeStruct((B, S, H * MLA_V), BF16)],
        compiler_params=_cparams("arbitrary", "arbitrary"),
        name="mla_prep",
    )(proj, cos, sin, qn, kn, wq, wkv)


def _mla_attn_kernel(q_ref, k_ref, v_ref, o_ref, *, tk):
    tq = q_ref.shape[1]
    q0 = pl.program_id(2) * tq
    q = q_ref[0]
    init = (jnp.full((1, tq, 1), MASKED, F32), jnp.zeros((1, tq, 1), F32), jnp.zeros((1, tq, MLA_V), F32))
    everything = jnp.full((tq, tk), True)

    def body(masked, kt, carry):
        k0 = pl.multiple_of(kt * tk, tk)
        s = _dot_t(q, k_ref[0, pl.ds(k0, tk), :])[None]
        if masked:
            mask = (k0 + lax.broadcasted_iota(jnp.int32, (tq, tk), 1)) <= (q0 + lax.broadcasted_iota(jnp.int32, (tq, tk), 0))
        else:
            mask = everything
        return _online_step(s, mask, v_ref[0, pl.ds(k0, tk), :], carry)

    n_full = q0 // tk
    n_all = (q0 + tq + tk - 1) // tk
    carry = lax.fori_loop(0, n_full, functools.partial(body, False), init)
    _, l, acc = lax.fori_loop(n_full, n_all, functools.partial(body, True), carry)
    o_ref[0] = (acc / l)[0]


def mla_attention(q, k, v, tq=256, tk=512):
    B, S, _ = q.shape
    H = MLA_HEADS
    tq, tk = min(tq, S), min(tk, S)
    return pl.pallas_call(
        functools.partial(_mla_attn_kernel, tk=tk),
        grid=(B, H, S // tq),
        in_specs=[pl.BlockSpec((1, tq, MLA_QK), lambda b, h, i: (b, i, h)),
                  pl.BlockSpec((1, S, MLA_QK), lambda b, h, i: (b, 0, h)),
                  pl.BlockSpec((1, S, MLA_V), lambda b, h, i: (b, 0, h))],
        out_specs=pl.BlockSpec((1, tq, MLA_V), lambda b, h, i: (b, i, h)),
        out_shape=jax.ShapeDtypeStruct((B, S, H * MLA_V), F32),
        compiler_params=_cparams("arbitrary", "arbitrary", "arbitrary"),
        name="mla_attention",
    )(q, k, v)


S5_CHUNK = 16
S5_GPB = LANES // S5_GROUP
HI = lax.Precision.HIGHEST


def s5_params(a_re, a_im, b_re, b_im, c_re, c_im, log_step, n_chunks):
    G, P = a_re.shape
    L = S5_CHUNK
    CI = b_re.shape[-1]
    CO = c_re.shape[1]
    NB = G // S5_GPB
    step = jnp.exp(log_step.astype(F32))[:, None]
    are, aim = a_re.astype(F32), a_im.astype(F32)
    mag = jnp.exp(are * step)
    lb_re, lb_im = mag * jnp.cos(aim * step), mag * jnp.sin(aim * step)
    den = are * are + aim * aim
    nr, ni = lb_re - 1.0, lb_im
    g_re = (nr * are + ni * aim) / den
    g_im = (ni * are - nr * aim) / den
    br, bi = b_re.astype(F32), b_im.astype(F32)
    bb_re = g_re[..., None] * br - g_im[..., None] * bi
    bb_im = g_re[..., None] * bi + g_im[..., None] * br

    def lam_pow(d):
        d = jnp.asarray(d, F32)[:, None, None]
        pmag = jnp.exp(are * step * d)
        return pmag * jnp.cos(aim * step * d), pmag * jnp.sin(aim * step * d)

    pw_re, pw_im = lam_pow(np.arange(L + 1))
    m_re = pw_re[..., None] * bb_re - pw_im[..., None] * bb_im
    m_im = pw_re[..., None] * bb_im + pw_im[..., None] * bb_re
    cr, ci_ = c_re.astype(F32), c_im.astype(F32)
    kd = (jnp.einsum('gop,dgpi->dgoi', cr, m_re[:L], precision=HI)
          - jnp.einsum('gop,dgpi->dgoi', ci_, m_im[:L], precision=HI))
    s_idx = np.arange(L)[:, None]
    t_idx = np.arange(L)[None, :]
    dmat = np.clip(t_idx - s_idx, 0, L - 1)
    eye = jnp.eye(S5_GPB, dtype=F32)
    kbd = jnp.einsum('dbgoi,gh->bdgiho', kd.reshape(L, NB, S5_GPB, CO, CI), eye).reshape(NB, L, LANES, LANES)
    tb = jnp.where((t_idx >= s_idx)[None, :, :, None, None], kbd[:, dmat], 0.0)
    T = tb.transpose(0, 1, 3, 2, 4).reshape(NB, L * LANES, L * LANES).astype(BF16)

    def to_p(m):
        m = m[::-1].reshape(L, NB, S5_GPB, P, CI)
        return jnp.einsum('sbgpi,gh->bsgihp', m, eye).astype(BF16).reshape(NB, L * LANES, S5_GPB * P)

    q_re = cr[None] * pw_re[1:, :, None, :] - ci_[None] * pw_im[1:, :, None, :]
    q_im = -(cr[None] * pw_im[1:, :, None, :] + ci_[None] * pw_re[1:, :, None, :])

    def to_q(m):
        m = m.reshape(L, NB, S5_GPB, CO, P)
        return jnp.einsum('tbgop,gh->bgptho', m, eye).astype(BF16).reshape(NB, S5_GPB * P, L * LANES)

    levels = max(1, int(np.ceil(np.log2(n_chunks))))
    lv_re, lv_im = lam_pow(L * 2 ** np.arange(levels))
    lam_lv = jnp.concatenate([lv_re.reshape(levels, NB, S5_GPB * P), lv_im.reshape(levels, NB, S5_GPB * P)], axis=0)
    lam_lv = lam_lv.transpose(1, 0, 2)
    return dict(T=T, p_re=to_p(m_re[:L]), p_im=to_p(m_im[:L]), q_re=to_q(q_re), q_im=to_q(q_im), lam=lam_lv)


def _s5_kernel(u_ref, t_ref, pre_ref, pim_ref, qre_ref, qim_ref, lam_ref, y_ref):
    L = S5_CHUNK
    nc = u_ref.shape[1] // L
    u = jnp.concatenate([u_ref[0, pl.ds(s, nc, stride=L), :].astype(BF16) for s in range(L)], axis=1)
    xr = jnp.dot(u, pre_ref[0], preferred_element_type=F32)
    xi = jnp.dot(u, pim_ref[0], preferred_element_type=F32)
    lam = lam_ref[0]
    levels = lam.shape[0] // 2
    row = lax.broadcasted_iota(jnp.int32, xr.shape, 0)
    for k in range(levels):
        sh = 2 ** k
        if sh >= nc:
            break
        lr, li = lam[k:k + 1, :], lam[levels + k:levels + k + 1, :]
        sr = jnp.where(row >= sh, pltpu.roll(xr, sh, axis=0), 0.0)
        si = jnp.where(row >= sh, pltpu.roll(xi, sh, axis=0), 0.0)
        xr, xi = xr + lr * sr - li * si, xi + lr * si + li * sr
    pr = jnp.where(row >= 1, pltpu.roll(xr, 1, axis=0), 0.0).astype(BF16)
    pi = jnp.where(row >= 1, pltpu.roll(xi, 1, axis=0), 0.0).astype(BF16)
    for t in range(0, L, 2):
        rows, cols = (t + 2) * LANES, slice(t * LANES, (t + 2) * LANES)
        y = (jnp.dot(u[:, :rows], t_ref[0, :rows, cols], preferred_element_type=F32)
             + jnp.dot(pr, qre_ref[0, :, cols], preferred_element_type=F32)
             + jnp.dot(pi, qim_ref[0, :, cols], preferred_element_type=F32))
        y_ref[0, pl.ds(t, nc, stride=L), :] = y[:, :LANES]
        y_ref[0, pl.ds(t + 1, nc, stride=L), :] = y[:, LANES:]


def s5_scan(proj, prm):
    B, S, _ = proj.shape
    L = S5_CHUNK
    NB = GROUP_W // LANES
    W = L * LANES
    ns = S5_GPB * S5_STATE
    op = lambda r, c: pl.BlockSpec((1, r, c), lambda n, b: (n, 0, 0))
    return pl.pallas_call(
        _s5_kernel,
        grid=(NB, B),
        in_specs=[pl.BlockSpec((1, S, LANES), lambda n, b: (b, 0, BLK_S5 * NB + n)),
                  op(W, W), op(W, ns), op(W, ns), op(ns, W), op(ns, W), op(prm["lam"].shape[1], ns)],
        out_specs=pl.BlockSpec((1, S, LANES), lambda n, b: (b, 0, n)),
        out_shape=jax.ShapeDtypeStruct((B, S, GROUP_W), F32),
        compiler_params=_cparams("arbitrary", "arbitrary"),
        name="s5_scan",
    )(proj, prm["T"], prm["p_re"], prm["p_im"], prm["q_re"], prm["q_im"], prm["lam"])


def _s5_post_kernel(y_ref, u_ref, d_ref, w_ref, b_ref, o_ref):
    y = jax.nn.gelu(y_ref[0] + d_ref[...] * u_ref[0])
    z = jnp.dot(y.astype(BF16), w_ref[...], preferred_element_type=F32) + b_ref[...]
    o_ref[0] = y * jax.nn.sigmoid(z)


def s5_post(y, proj, d_skip, w_glu_bf16, b_glu, tm=512):
    B, S, CH = y.shape
    tm = min(tm, S)
    blk = pl.BlockSpec((1, tm, CH), lambda b, i: (b, i, 0))
    vec = pl.BlockSpec((1, CH), lambda b, i: (0, 0))
    return pl.pallas_call(
        _s5_post_kernel,
        grid=(B, S // tm),
        in_specs=[blk, pl.BlockSpec((1, tm, CH), lambda b, i: (b, i, BLK_S5)), vec,
                  pl.BlockSpec((CH, CH), lambda b, i: (0, 0)), vec],
        out_specs=blk,
        out_shape=jax.ShapeDtypeStruct((B, S, CH), F32),
        compiler_params=_cparams("arbitrary", "arbitrary"),
        name="s5_post",
    )(y, proj, d_skip.reshape(1, CH), w_glu_bf16, b_glu.reshape(1, CH))


def _nsa_prep_kernel(kv_ref, cos_ref, sin_ref, kc_ref, vc_ref, ks_ref, vs_ref, kw_ref, vw_ref):
    cos, sin = cos_ref[0], sin_ref[0]
    outs = (kc_ref, vc_ref, ks_ref, vs_ref, kw_ref, vw_ref)
    for n, o_ref in enumerate(outs):
        x = kv_ref[0, :, n * NSA_DIM:(n + 1) * NSA_DIM]
        if n % 2 == 0:
            x = _rope(x, cos, sin, NSA_ROT // 2)
        o_ref[0] = x.astype(BF16)


def nsa_prep(proj, cos, sin, tm=512):
    B, S, _ = proj.shape
    tm = min(tm, S)
    one = pl.BlockSpec((1, tm, NSA_DIM), lambda b, i: (b, i, 0))
    return pl.pallas_call(
        _nsa_prep_kernel,
        grid=(B, S // tm),
        in_specs=[pl.BlockSpec((1, tm, 6 * NSA_DIM), lambda b, i: (b, i, BLK_NSA_KV)), one, one],
        out_specs=[one] * 6,
        out_shape=[jax.ShapeDtypeStruct((B, S, NSA_DIM), BF16)] * 6,
        compiler_params=_cparams("arbitrary", "arbitrary"),
        name="nsa_prep",
    )(proj, cos, sin)


def _nsa_compress_kernel(xk_ref, xv_ref, pk_ref, w1k_ref, w2k_ref, pv_ref, w1v_ref, w2v_ref, ok_ref, ov_ref):
    for x_ref, p_ref, w1_ref, w2_ref, o_ref in ((xk_ref, pk_ref, w1k_ref, w2k_ref, ok_ref),
                                                (xv_ref, pv_ref, w1v_ref, w2v_ref, ov_ref)):
        x = x_ref[0]
        half = x.shape[1]
        a = jnp.dot(x, w1_ref[0:half, :], preferred_element_type=F32)
        b = jnp.dot(x, w1_ref[half:, :], preferred_element_type=F32)
        c = jnp.dot(p_ref[...], w1_ref[...], preferred_element_type=F32)[0:1, :]
        pre = a + pltpu.roll(b, x.shape[0] - 1, axis=0) + c
        h = jax.nn.gelu(pre)
        o_ref[0] = jnp.dot(h.astype(BF16), w2_ref[...], preferred_element_type=F32).astype(BF16)


def nsa_compress(kc, vc, pos_k, w1k, w2k, pos_v, w1v, w2v):
    B, S, D = kc.shape
    nseg = S // CMP_STRIDE
    seg_w = CMP_STRIDE * D
    xk = kc.reshape(B, nseg, seg_w)
    xv = vc.reshape(B, nseg, seg_w)

    def posrow(p):
        return jnp.broadcast_to(p.reshape(1, CMP_BLOCK * D), (SUBLANES, CMP_BLOCK * D)).astype(BF16)

    xs = pl.BlockSpec((1, nseg, seg_w), lambda b: (b, 0, 0))
    ps = pl.BlockSpec((SUBLANES, CMP_BLOCK * D), lambda b: (0, 0))
    w1s = pl.BlockSpec((CMP_BLOCK * D, D), lambda b: (0, 0))
    w2s = pl.BlockSpec((D, D), lambda b: (0, 0))
    os_ = pl.BlockSpec((1, nseg, D), lambda b: (b, 0, 0))
    return pl.pallas_call(
        _nsa_compress_kernel,
        grid=(B,),
        in_specs=[xs, xs, ps, w1s, w2s, ps, w1s, w2s],
        out_specs=[os_, os_],
        out_shape=[jax.ShapeDtypeStruct((B, nseg, D), BF16)] * 2,
        compiler_params=_cparams("arbitrary"),
        name="nsa_compress",
    )(xk, xv, posrow(pos_k), w1k.astype(BF16), w2k.astype(BF16), posrow(pos_v), w1v.astype(BF16), w2v.astype(BF16))


def _nsa_attn_kernel(q_ref, gate_ref, cos_ref, sin_ref, kc_ref, vc_ref, ks_ref, vs_ref, kw_ref, vw_ref, o_ref,
                     *, tk_sel):
    H, D = NSA_HEADS, NSA_DIM
    tq = q_ref.shape[1]
    S = ks_ref.shape[1]
    n_cmp = kc_ref.shape[1]
    n_sel = S // SEL_BLOCK
    q0 = pl.program_id(1) * tq
    scale = D ** -0.5
    cos, sin = cos_ref[0], sin_ref[0]
    q = jnp.concatenate(
        [(_rope(q_ref[0, :, h * D:(h + 1) * D], cos, sin, NSA_ROT // 2) * scale).astype(BF16) for h in range(H)],
        axis=0)

    t_c = q0 + lax.broadcasted_iota(jnp.int32, (tq, n_cmp), 0)
    n_c = lax.broadcasted_iota(jnp.int32, (tq, n_cmp), 1)
    valid = (t_c >= n_c * CMP_STRIDE + (CMP_BLOCK - 1)) & (n_c < n_cmp - 1)
    s = _dot_t(q, kc_ref[0]).reshape(H, tq, n_cmp)
    s = jnp.where(valid[None], s, MASKED)
    e = jnp.exp(s - jnp.max(s, axis=-1, keepdims=True))
    p = jnp.where(valid[None], e / jnp.sum(e, axis=-1, keepdims=True), 0.0)
    o_cmp = jnp.dot(p.reshape(H * tq, n_cmp).astype(BF16), vc_ref[0], preferred_element_type=F32).reshape(H, tq, D)

    psum = jnp.sum(p, axis=0)
    n_o = lax.broadcasted_iota(jnp.int32, (n_cmp, n_sel), 0) * CMP_STRIDE
    j_o = lax.broadcasted_iota(jnp.int32, (n_cmp, n_sel), 1) * SEL_BLOCK
    overlap = jnp.where((n_o < j_o + SEL_BLOCK) & (n_o + CMP_BLOCK > j_o), 1.0, 0.0).astype(BF16)
    p_hi = psum.astype(BF16)
    p_lo = (psum - p_hi.astype(F32)).astype(BF16)
    imp = (jnp.dot(p_hi, overlap, preferred_element_type=F32) + jnp.dot(p_lo, overlap, preferred_element_type=F32))
    t_s = q0 + lax.broadcasted_iota(jnp.int32, (tq, n_sel), 0)
    j_i = lax.broadcasted_iota(jnp.int32, (tq, n_sel), 1)
    t_blk = t_s // SEL_BLOCK
    forced = (j_i == 0) | (j_i == t_blk) | (j_i == t_blk - 1)
    imp = jnp.where(forced, 1e6, jnp.where(j_i > t_blk, -1e6, imp))
    j_s = j_i.astype(F32)
    sel = jnp.zeros((tq, n_sel), F32)
    for _ in range(min(SEL_TOPK, n_sel)):
        mx = jnp.max(imp, axis=-1, keepdims=True)
        first = jnp.min(jnp.where(imp == mx, j_s, float(n_sel)), axis=-1, keepdims=True)
        hit = j_s == first
        sel = jnp.where(hit, 1.0, sel)
        imp = jnp.where(hit, -jnp.inf, imp)
    sel_b = sel.astype(BF16)

    init = (jnp.full((H, tq, 1), MASKED, F32), jnp.zeros((H, tq, 1), F32), jnp.zeros((H, tq, D), F32))

    def sel_body(kt, carry):
        k0 = pl.multiple_of(kt * tk_sel, tk_sel)
        kblk = ks_ref[0, pl.ds(k0, tk_sel), :]
        vblk = vs_ref[0, pl.ds(k0, tk_sel), :]
        sc = _dot_t(q, kblk).reshape(H, tq, tk_sel)
        j_e = lax.broadcasted_iota(jnp.int32, (n_sel, tk_sel), 0)
        c_e = (k0 + lax.broadcasted_iota(jnp.int32, (n_sel, tk_sel), 1)) // SEL_BLOCK
        expand = jnp.where(j_e == c_e, 1.0, 0.0).astype(BF16)
        chosen = jnp.dot(sel_b, expand, preferred_element_type=F32)
        kpos = k0 + lax.broadcasted_iota(jnp.int32, (tq, tk_sel), 1)
        tpos = q0 + lax.broadcasted_iota(jnp.int32, (tq, tk_sel), 0)
        return _online_step(sc, (chosen > 0.5) & (kpos <= tpos), vblk, carry)

    n_kt = (q0 + tq + tk_sel - 1) // tk_sel
    _, l_s, acc_s = lax.fori_loop(0, n_kt, sel_body, init)
    o_slc = acc_s / l_s

    n_w = WINDOW // tq + 1
    first_w = jnp.maximum(n_w - 1 - pl.program_id(1), 0)

    def win_body(w, carry):
        k0 = pl.multiple_of(q0 - WINDOW + w * tq, tq)
        kblk = kw_ref[0, pl.ds(k0, tq), :]
        vblk = vw_ref[0, pl.ds(k0, tq), :]
        sc = _dot_t(q, kblk).reshape(H, tq, tq)
        rel = (q0 + lax.broadcasted_iota(jnp.int32, (tq, tq), 0)) - (k0 + lax.broadcasted_iota(jnp.int32, (tq, tq), 1))
        return _online_step(sc, (rel >= 0) & (rel < WINDOW), vblk, carry)

    _, l_w, acc_w = lax.fori_loop(first_w, n_w, win_body, init)
    o_win = acc_w / l_w

    gates = jax.nn.sigmoid(gate_ref[0])
    for h in range(H):
        o_ref[0, :, h * D:(h + 1) * D] = (gates[:, h:h + 1] * o_cmp[h] + gates[:, H + h:H + h + 1] * o_slc[h]
                                          + gates[:, 2 * H + h:2 * H + h + 1] * o_win[h])


def nsa_attention(proj, cos, sin, kcmp, vcmp, ks, vs, kw, vw, tq=128):
    B, S, _ = proj.shape
    D = NSA_DIM
    n_cmp = kcmp.shape[1]
    assert S % tq == 0 and WINDOW % tq == 0
    tk_sel = min(256, S)
    tile = pl.BlockSpec((1, tq, D), lambda b, i: (b, i, 0))
    seq = pl.BlockSpec((1, S, D), lambda b, i: (b, 0, 0))
    cmp_ = pl.BlockSpec((1, n_cmp, D), lambda b, i: (b, 0, 0))
    return pl.pallas_call(
        functools.partial(_nsa_attn_kernel, tk_sel=tk_sel),
        grid=(B, S // tq),
        in_specs=[pl.BlockSpec((1, tq, NSA_HEADS * D), lambda b, i: (b, i, BLK_NSA_Q)),
                  pl.BlockSpec((1, tq, LANES), lambda b, i: (b, i, BLK_SMALL)),
                  tile, tile, cmp_, cmp_, seq, seq, seq, seq],
        out_specs=pl.BlockSpec((1, tq, NSA_HEADS * D), lambda b, i: (b, i, 0)),
        out_shape=jax.ShapeDtypeStruct((B, S, NSA_HEADS * D), F32),
        compiler_params=_cparams("arbitrary", "arbitrary"),
        name="nsa_attention",
    )(proj, proj, cos, sin, kcmp, vcmp, ks, vs, kw, vw)


GDN_W = GDN_HEADS * GDN_DIM


def _causal_conv_silu(x, halo, w_ref, c0):
    taps = w_ref.shape[0]
    row = lax.broadcasted_iota(jnp.int32, x.shape, 0)
    acc = x * w_ref[taps - 1:taps, c0:c0 + x.shape[1]]
    for d in range(1, taps):
        shifted = pltpu.roll(x, d, axis=0)
        for r in range(d):
            shifted = jnp.where(row == r, halo[SUBLANES - d + r:SUBLANES - d + r + 1, :], shifted)
        acc = acc + shifted * w_ref[taps - 1 - d:taps - d, c0:c0 + x.shape[1]]
    return acc * jax.nn.sigmoid(acc)


def _gdn_prep_kernel(q_ref, k_ref, v_ref, hq_ref, hk_ref, hv_ref, sm_ref, w_ref, a_ref, dt_ref,
                     qo_ref, ko_ref, vo_ref, go_ref):
    first = pl.program_id(1) == 0
    for n, (x_ref, h_ref, o_ref) in enumerate(((q_ref, hq_ref, qo_ref), (k_ref, hk_ref, ko_ref), (v_ref, hv_ref, vo_ref))):
        halo = jnp.where(first, 0.0, h_ref[0])
        y = _causal_conv_silu(x_ref[0], halo, w_ref, n * GDN_W)
        for h in range(GDN_HEADS):
            yh = y[:, h * GDN_DIM:(h + 1) * GDN_DIM]
            if n < 2:
                yh = yh * lax.rsqrt(jnp.sum(yh * yh, axis=-1, keepdims=True) + 1e-6)
            if n == 0:
                yh = yh * GDN_DIM ** -0.5
            o_ref[0, :, h * GDN_DIM:(h + 1) * GDN_DIM] = yh
    sm = sm_ref[0]
    lane = lax.broadcasted_iota(jnp.int32, sm.shape, 1)
    g = -a_ref[...] * jax.nn.softplus(sm + dt_ref[...])
    go_ref[0] = jnp.where((lane >= GA_LANE) & (lane < GA_LANE + GDN_HEADS), g, jax.nn.sigmoid(sm))


def gdn_prep(proj, conv_w, a_log, dt_bias, tm=256):
    B, S, _ = proj.shape
    tm = min(tm, S)
    rb = tm // SUBLANES
    a_row = jnp.zeros((1, LANES), F32).at[0, GA_LANE:GA_LANE + GDN_HEADS].set(jnp.exp(a_log.astype(F32)))
    dt_row = jnp.zeros((1, LANES), F32).at[0, GA_LANE:GA_LANE + GDN_HEADS].set(dt_bias.astype(F32))

    def main(n):
        return pl.BlockSpec((1, tm, GDN_W), lambda b, i: (b, i, BLK_GDN_Q + n))

    def halo(n):
        return pl.BlockSpec((1, SUBLANES, GDN_W), lambda b, i: (b, jnp.maximum(i * rb - 1, 0), BLK_GDN_Q + n))

    out = pl.BlockSpec((1, tm, GDN_W), lambda b, i: (b, i, 0))
    sm = pl.BlockSpec((1, tm, LANES), lambda b, i: (b, i, BLK_SMALL))
    row = pl.BlockSpec((1, LANES), lambda b, i: (0, 0))
    return pl.pallas_call(
        _gdn_prep_kernel,
        grid=(B, S // tm),
        in_specs=[main(0), main(1), main(2), halo(0), halo(1), halo(2), sm,
                  pl.BlockSpec(conv_w.shape, lambda b, i: (0, 0)), row, row],
        out_specs=[out, out, out, pl.BlockSpec((1, tm, LANES), lambda b, i: (b, i, 0))],
        out_shape=[jax.ShapeDtypeStruct((B, S, GDN_W), F32)] * 3 + [jax.ShapeDtypeStruct((B, S, LANES), F32)],
        compiler_params=_cparams("arbitrary", "arbitrary"),
        name="gdn_prep",
    )(proj, proj, proj, proj, proj, proj, proj, conv_w, a_row, dt_row)


def _split3(x):
    hi = x.astype(BF16)
    r = x - hi.astype(F32)
    mid = r.astype(BF16)
    return hi, mid, (r - mid.astype(F32)).astype(BF16)


def _bmm(a, b):
    return jnp.einsum('nik,nkj->nij', a.astype(BF16), b.astype(BF16), preferred_element_type=F32)


def _bmm_t(a, b):
    return jnp.einsum('nik,njk->nij', a.astype(BF16), b.astype(BF16), preferred_element_type=F32)


def _gdn_core_kernel(q_ref, k_ref, v_ref, g_ref, grow_ref, z_ref, on_ref, o_ref, st_ref):
    C, D, H = GDN_CHUNK, GDN_DIM, GDN_HEADS
    tc = q_ref.shape[1]
    nc = tc // C

    @pl.when(pl.program_id(1) == 0)
    def _():
        st_ref[...] = jnp.zeros_like(st_ref)

    def per_head(ref):
        x = ref[0]
        return jnp.stack([x[c * C:(c + 1) * C, h * D:(h + 1) * D] for c in range(nc) for h in range(H)], axis=0)

    ri = lax.broadcasted_iota(jnp.int32, (C, C), 0)
    ci = lax.broadcasted_iota(jnp.int32, (C, C), 1)
    incl = (ri >= ci)[None]
    strict = (ri > ci)[None]
    eye = jnp.where(ri == ci, 1.0, 0.0)[None]
    upper_ones = jnp.where(ri <= ci, 1.0, 0.0).astype(BF16)
    rt = lax.broadcasted_iota(jnp.int32, (tc, tc), 0)
    ct = lax.broadcasted_iota(jnp.int32, (tc, tc), 1)
    chunk_lower = jnp.where((rt >= ct) & (rt // C == ct // C), 1.0, 0.0).astype(BF16)

    gates = g_ref[0]
    gc_all = sum(jnp.dot(chunk_lower, p, preferred_element_type=F32) for p in _split3(gates))
    grow = grow_ref[0].reshape(nc * SUBLANES, C)
    gr_all = sum(jnp.dot(p, upper_ones, preferred_element_type=F32) for p in _split3(grow))

    def col(x, lane0):
        return jnp.stack([x[c * C:(c + 1) * C, lane0 + h:lane0 + h + 1] for c in range(nc) for h in range(H)], axis=0)

    gc = col(gc_all, GA_LANE)
    beta = col(gates, GB_LANE)
    gr = jnp.stack([gr_all[c * SUBLANES + h:c * SUBLANES + h + 1, :] for c in range(nc) for h in range(H)], axis=0)
    q, k, v = per_head(q_ref), per_head(k_ref), per_head(v_ref)

    decay = jnp.where(incl, jnp.exp(jnp.where(incl, gc - gr, 0.0)), 0.0)
    kb = k * beta
    egc = jnp.exp(gc)
    low = jnp.where(strict, _bmm_t(kb, k) * decay, 0.0)
    s_ = 1
    inv = None
    while s_ < C:
        pm = ((ri // (2 * s_) == ci // (2 * s_)) & (ri % (2 * s_) >= s_) & (ci % (2 * s_) < s_))[None]
        off = jnp.where(pm, low, 0.0)
        inv = eye - off if inv is None else inv - _bmm(inv, _bmm(off, inv))
        s_ *= 2
    x = _bmm(inv, jnp.concatenate([v * beta, kb * egc], axis=2))
    u, w = x[:, :, :D], x[:, :, D:]
    intra = jnp.where(incl, _bmm_t(q, k) * decay, 0.0)
    g_last = gc[:, C - 1:C, :]
    q_dec = q * egc
    k_dec = k * jnp.exp(g_last - gc)
    e_last = jnp.exp(g_last)

    st = [st_ref[h] for h in range(H)]
    for c in range(nc):
        for h in range(H):
            i = c * H + h
            v_new = u[i] - _dot_b(w[i], st[h])
            o = _dot_b(q_dec[i], st[h]) + _dot_b(intra[i], v_new)
            st[h] = st[h] * e_last[i] + lax.dot_general(
                k_dec[i].astype(BF16), v_new.astype(BF16), (((0,), (0,)), ((), ())), preferred_element_type=F32)
            zh = z_ref[0, c * C:(c + 1) * C, h * D:(h + 1) * D]
            o_ref[0, c * C:(c + 1) * C, h * D:(h + 1) * D] = _rms(o, on_ref[...]) * (zh * jax.nn.sigmoid(zh))
    for h in range(H):
        st_ref[h] = st[h]


def gdn_core(q, k, v, gates, proj, o_norm, tc=256):
    B, S, _ = q.shape
    C = GDN_CHUNK
    tc = min(tc, S)
    g_rows = gates[..., GA_LANE:GA_LANE + GDN_HEADS].reshape(B, S // C, C, GDN_HEADS).transpose(0, 1, 3, 2)
    g_rows = jnp.pad(g_rows, ((0, 0), (0, 0), (0, SUBLANES - GDN_HEADS), (0, 0)))
    blk = pl.BlockSpec((1, tc, GDN_W), lambda b, i: (b, i, 0))
    return pl.pallas_call(
        _gdn_core_kernel,
        grid=(B, S // tc),
        in_specs=[blk, blk, blk,
                  pl.BlockSpec((1, tc, LANES), lambda b, i: (b, i, 0)),
                  pl.BlockSpec((1, tc // C, SUBLANES, C), lambda b, i: (b, i, 0, 0)),
                  pl.BlockSpec((1, tc, GDN_W), lambda b, i: (b, i, BLK_GDN_Z)),
                  pl.BlockSpec((1, GDN_DIM), lambda b, i: (0, 0))],
        out_specs=blk,
        out_shape=jax.ShapeDtypeStruct((B, S, GDN_W), F32),
        scratch_shapes=[pltpu.VMEM((GDN_HEADS, GDN_DIM, GDN_DIM), F32)],
        compiler_params=_cparams("arbitrary", "arbitrary"),
        name="gdn_core",
    )(q, k, v, gates, g_rows, proj, o_norm.reshape(1, GDN_DIM))


def kernel(x, c, positions, w_ada, b_ada, norm_pre_mix, norm_post_mix, norm_pre_ffn, norm_post_ffn, w_in, w_out, gn_mla, gn_s5, gn_nsa, mla_q_norm, mla_w_uq, mla_kv_norm, mla_w_ukv, s5_a_re, s5_a_im, s5_b_re, s5_b_im, s5_c_re, s5_c_im, s5_d, s5_log_step, s5_w_glu, s5_b_glu, nsa_pos_k, nsa_w1_k, nsa_w2_k, nsa_pos_v, nsa_w1_v, nsa_w2_v, gdn_conv_w, gdn_a_log, gdn_dt_bias, gdn_o_norm, ffn_w_in, ffn_conv_w, ffn_w_out):
    depth = w_in.shape[0]
    cos_mla, sin_mla = rope_tables(positions, MLA_ROPE // 2)
    cos_nsa, sin_nsa = rope_tables(positions, NSA_ROT // 2)
    for l in range(depth):
        mods = ada_mods(c, w_ada[l], b_ada[l])
        sh1, sc1, g1, sh2, sc2, g2 = jnp.split(mods, 6, axis=-1)

        proj = normmod_matmul(x, norm_pre_mix[l], sc1, sh1, permute_w_in(w_in[l]), F32, tm=512, tn=PROJ_COLS // 3)

        wq, wkv = mla_weights(mla_w_uq[l], mla_w_ukv[l])
        q_a, k_a, v_a = mla_prep(proj, cos_mla, sin_mla, mla_q_norm[l], mla_kv_norm[l], wq, wkv)
        o_a = mla_attention(q_a, k_a, v_a)

        s5p = s5_params(s5_a_re[l], s5_a_im[l], s5_b_re[l], s5_b_im[l], s5_c_re[l], s5_c_im[l], s5_log_step[l],
                        x.shape[1] // S5_CHUNK)
        y_b = s5_scan(proj, s5p)
        o_b = s5_post(y_b, proj, s5_d[l], s5_w_glu[l].astype(BF16), s5_b_glu[l])

        kc, vc, ks, vs, kw, vw = nsa_prep(proj, cos_nsa, sin_nsa)
        kcmp, vcmp = nsa_compress(kc, vc, nsa_pos_k[l], nsa_w1_k[l], nsa_w2_k[l], nsa_pos_v[l], nsa_w1_v[l], nsa_w2_v[l])
        o_c = nsa_attention(proj, cos_nsa, sin_nsa, kcmp, vcmp, ks, vs, kw, vw)

        q_d, k_d, v_d, gates_d = gdn_prep(proj, gdn_conv_w[l], gdn_a_log[l], gdn_dt_bias[l])
        o_d = gdn_core(q_d, k_d, v_d, gates_d, proj, gdn_o_norm[l])

        gn3 = jnp.stack([gn_mla[l], gn_s5[l], gn_nsa[l]])
        x = mix_out(o_a, o_b, o_c, o_d, gn3, w_out[l].astype(BF16), x, g1, norm_post_mix[l], tm=512)

        gv_ffn = normmod_matmul(x, norm_pre_ffn[l], sc2, sh2, ffn_w_in[l].astype(BF16), BF16, tm=1024, tn=512)
        x = ffn_out(gv_ffn, ffn_conv_w[l], ffn_w_out[l].astype(BF16), x, g2, norm_post_ffn[l], tm=512, tf=512)
    return x
```

```python
import functools

import jax
import jax.numpy as jnp
import numpy as np
from jax import lax
from jax.experimental import pallas as pl
from jax.experimental.pallas import tpu as pltpu

F32 = jnp.float32
BF16 = jnp.bfloat16

NORM_EPS = 1e-6
ROPE_THETA = 500000.0

MLA_HEADS, MLA_NOPE, MLA_ROPE, MLA_V, MLA_Q_RANK, MLA_KV_RANK = 4, 128, 64, 128, 384, 256
S5_GROUP, S5_STATE = 16, 64
NSA_HEADS, NSA_DIM = 4, 128
NSA_ROT = NSA_DIM // 4
CMP_BLOCK, CMP_STRIDE, SEL_BLOCK, SEL_TOPK, WINDOW = 32, 16, 64, 16, 512
GDN_HEADS, GDN_DIM, GDN_CONV, GDN_CHUNK = 4, 128, 4, 64
GROUP_W = 512

LANES = 128
SUBLANES = 8
VMEM_LIMIT_BYTES = 56 * 1024 * 1024
FFN_HALO = 16
MASKED = -1e30

PROJ_COLS = 4992
BLK_S5, BLK_NSA_Q, BLK_GDN_Q, BLK_GDN_Z = 0, 1, 2, 5
BLK_MLA = 4
BLK_NSA_KV = 5
BLK_SMALL = 36
GA_LANE, GB_LANE = 12, 16


def _cparams(*sem):
    return pltpu.CompilerParams(dimension_semantics=sem, vmem_limit_bytes=VMEM_LIMIT_BYTES)


def _rms(x, g):
    return x * lax.rsqrt(jnp.mean(x * x, axis=-1, keepdims=True) + NORM_EPS) * g


def _dot_t(a, b):
    return lax.dot_general(a, b, (((1,), (1,)), ((), ())), preferred_element_type=F32)


def _dot_b(a, b):
    return jnp.dot(a.astype(BF16), b.astype(BF16), preferred_element_type=F32)


def _cast_kernel(w_ref, o_ref):
    o_ref[...] = w_ref[0].astype(BF16)


def cast_layer(w, l, tr=256):
    _, R, C = w.shape
    tr = min(tr, R)
    tc = next((t for t in (2816, 2048, 1024) if C % t == 0), C)
    return pl.pallas_call(
        _cast_kernel,
        grid=(R // tr, C // tc),
        in_specs=[pl.BlockSpec((1, tr, tc), lambda i, j: (l, i, j))],
        out_specs=pl.BlockSpec((tr, tc), lambda i, j: (i, j)),
        out_shape=jax.ShapeDtypeStruct((R, C), BF16),
        compiler_params=_cparams("arbitrary", "arbitrary"),
        name="cast_layer",
    )(w)


def _w_in_moves():
    sizes = (MLA_Q_RANK, MLA_KV_RANK, MLA_ROPE, GROUP_W, GROUP_W) + (NSA_DIM,) * 6 + (3 * NSA_HEADS,) \
        + (GROUP_W,) * 4 + (GDN_HEADS, GDN_HEADS)
    names = ("cq", "ckv", "kpe", "u_s5", "nq", "kc", "vc", "ks", "vs", "kw", "vw", "ngate", "gq", "gk", "gv", "gz", "ga", "gb")
    src = dict(zip(names, np.concatenate([[0], np.cumsum(sizes)[:-1]]).tolist()))
    width = dict(zip(names, sizes))
    order = ("u_s5", "nq", "gq", "gk", "gv", "gz", "cq", "ckv", "kpe", None, "kc", "vc", "ks", "vs", "kw", "vw", "ngate", "ga", "gb")
    moves, dst = [], 0
    for n in order:
        if n is None:
            dst += LANES - MLA_ROPE
            continue
        moves.append((src[n], dst, width[n]))
        dst += width[n]
    return moves


def _permute_kernel(w_ref, o_ref):
    o_ref[...] = jnp.zeros_like(o_ref)
    for s, d, w in _w_in_moves():
        o_ref[:, d:d + w] = w_ref[0, :, s:s + w].astype(BF16)


def permute_w_in(w_in, l, tr=256):
    _, D, C = w_in.shape
    tr = min(tr, D)
    return pl.pallas_call(
        _permute_kernel,
        grid=(D // tr,),
        in_specs=[pl.BlockSpec((1, tr, C), lambda i: (l, i, 0))],
        out_specs=pl.BlockSpec((tr, PROJ_COLS), lambda i: (i, 0)),
        out_shape=jax.ShapeDtypeStruct((D, PROJ_COLS), BF16),
        compiler_params=_cparams("arbitrary"),
        name="permute_w_in",
    )(w_in)


def _ada_kernel(c_ref, w_ref, b_ref, o_ref):
    cond = jax.nn.silu(c_ref[...]).astype(BF16)
    o_ref[...] = jnp.dot(cond, w_ref[0].astype(BF16), preferred_element_type=F32) + b_ref[0]


def ada_mods(c, w_ada, b_ada, l):
    B, D = c.shape
    L, _, N = w_ada.shape
    c_pad = jnp.zeros((SUBLANES, D), F32).at[:B].set(c)
    tn = 1024
    assert N % tn == 0 and B <= SUBLANES
    out = pl.pallas_call(
        _ada_kernel,
        grid=(N // tn,),
        in_specs=[pl.BlockSpec((SUBLANES, D), lambda j: (0, 0)),
                  pl.BlockSpec((1, D, tn), lambda j: (l, 0, j)),
                  pl.BlockSpec((1, 1, tn), lambda j: (l, 0, j))],
        out_specs=pl.BlockSpec((SUBLANES, tn), lambda j: (0, j)),
        out_shape=jax.ShapeDtypeStruct((SUBLANES, N), F32),
        compiler_params=_cparams("arbitrary"),
        name="ada_mods",
    )(c_pad, w_ada, b_ada.reshape(L, 1, N))
    return out[:B]


def _normmod_matmul_kernel(x_ref, g_ref, sc_ref, sh_ref, w_ref, o_ref, h_ref):
    @pl.when(pl.program_id(2) == 0)
    def _():
        h_ref[...] = (_rms(x_ref[0], g_ref[...]) * (1.0 + sc_ref[0]) + sh_ref[0]).astype(BF16)

    o_ref[0] = jnp.dot(h_ref[...], w_ref[...], preferred_element_type=F32).astype(o_ref.dtype)


def normmod_matmul(x, gain, sc, sh, w_bf16, out_dtype, tm, tn):
    B, S, D = x.shape
    N = w_bf16.shape[1]
    tm = min(tm, S)
    assert S % tm == 0 and N % tn == 0
    return pl.pallas_call(
        _normmod_matmul_kernel,
        grid=(B, S // tm, N // tn),
        in_specs=[pl.BlockSpec((1, tm, D), lambda b, i, j: (b, i, 0)),
                  pl.BlockSpec((1, D), lambda b, i, j: (0, 0)),
                  pl.BlockSpec((1, 1, D), lambda b, i, j: (b, 0, 0)),
                  pl.BlockSpec((1, 1, D), lambda b, i, j: (b, 0, 0)),
                  pl.BlockSpec((D, tn), lambda b, i, j: (0, j))],
        out_specs=pl.BlockSpec((1, tm, tn), lambda b, i, j: (b, i, j)),
        out_shape=jax.ShapeDtypeStruct((B, S, N), out_dtype),
        scratch_shapes=[pltpu.VMEM((tm, D), BF16)],
        compiler_params=_cparams("arbitrary", "arbitrary", "arbitrary"),
        name="normmod_matmul",
    )(x, gain.reshape(1, D), sc.reshape(B, 1, D), sh.reshape(B, 1, D), w_bf16)


def _mix_out_kernel(oa_ref, ob_ref, oc_ref, od_ref, gn_ref, w_ref, x_ref, gate_ref, gamma_ref, o_ref):
    gw = oa_ref.shape[-1]
    parts = (_rms(oa_ref[0], gn_ref[0:1, :]), _rms(ob_ref[0], gn_ref[1:2, :]),
             _rms(oc_ref[0], gn_ref[2:3, :]), od_ref[0])
    y = None
    for p, part in enumerate(parts):
        t = jnp.dot(part.astype(BF16), w_ref[p * gw:(p + 1) * gw, :], preferred_element_type=F32)
        y = t if y is None else y + t
    o_ref[0] = x_ref[0] + gate_ref[0] * _rms(y, gamma_ref[...])


def mix_out(o_a, o_b, o_c, o_d, gn3, w_out_bf16, x, gate, gamma, tm):
    B, S, D = x.shape
    gw = o_a.shape[-1]
    tm = min(tm, S)
    grp = pl.BlockSpec((1, tm, gw), lambda b, i: (b, i, 0))
    return pl.pallas_call(
        _mix_out_kernel,
        grid=(B, S // tm),
        in_specs=[grp, grp, grp, grp,
                  pl.BlockSpec((3, gw), lambda b, i: (0, 0)),
                  pl.BlockSpec((4 * gw, D), lambda b, i: (0, 0)),
                  pl.BlockSpec((1, tm, D), lambda b, i: (b, i, 0)),
                  pl.BlockSpec((1, 1, D), lambda b, i: (b, 0, 0)),
                  pl.BlockSpec((1, D), lambda b, i: (0, 0))],
        out_specs=pl.BlockSpec((1, tm, D), lambda b, i: (b, i, 0)),
        out_shape=jax.ShapeDtypeStruct((B, S, D), F32),
        compiler_params=_cparams("arbitrary", "arbitrary"),
        name="mix_out",
    )(o_a, o_b, o_c, o_d, gn3, w_out_bf16, x, gate.reshape(B, 1, D), gamma.reshape(1, D))


def _ffn_kernel(x_ref, xh_ref, gain_ref, sc_ref, sh_ref, wg_ref, wv_ref, cw_ref, wo_ref, g2_ref, gamma_ref, o_ref,
                h_ref, acc_ref):
    i, k = pl.program_id(1), pl.program_id(2)

    @pl.when(k == 0)
    def _():
        mod = lambda t: (_rms(t, gain_ref[...]) * (1.0 + sc_ref[0]) + sh_ref[0]).astype(BF16)
        h_ref[0:FFN_HALO, :] = mod(xh_ref[0])
        h_ref[FFN_HALO:, :] = mod(x_ref[0])
        acc_ref[...] = jnp.zeros_like(acc_ref)

    gate = jnp.dot(h_ref[...], wg_ref[...], preferred_element_type=F32)
    val = jnp.dot(h_ref[FFN_HALO:, :], wv_ref[...], preferred_element_type=F32)
    row = lax.broadcasted_iota(jnp.int32, gate.shape, 0)
    gate = jnp.where((row >= FFN_HALO) | (i > 0), gate, 0.0)
    prev1 = pltpu.roll(gate, 1, axis=0)[FFN_HALO:, :]
    prev2 = pltpu.roll(gate, 2, axis=0)[FFN_HALO:, :]
    conv = cw_ref[0:1, :] * prev2 + cw_ref[1:2, :] * prev1 + cw_ref[2:3, :] * gate[FFN_HALO:, :]
    a = (jax.nn.gelu(conv) * val).astype(BF16)
    acc_ref[...] += jnp.dot(a, wo_ref[...], preferred_element_type=F32)

    @pl.when(k == pl.num_programs(2) - 1)
    def _():
        o_ref[0] = x_ref[0] + g2_ref[0] * _rms(acc_ref[...], gamma_ref[...])


def ffn_fused(x, gain, sc, sh, w_in_bf16, conv_w, w_out_bf16, g2, gamma, tm=512, tf=512):
    B, S, D = x.shape
    F = w_out_bf16.shape[0]
    tm = min(tm, S)
    nk = F // tf
    assert F % tf == 0 and S % tm == 0 and tm % FFN_HALO == 0
    rb = tm // FFN_HALO
    vec = pl.BlockSpec((1, D), lambda b, i, k: (0, 0))
    bvec = pl.BlockSpec((1, 1, D), lambda b, i, k: (b, 0, 0))
    return pl.pallas_call(
        _ffn_kernel,
        grid=(B, S // tm, nk),
        in_specs=[pl.BlockSpec((1, tm, D), lambda b, i, k: (b, i, 0)),
                  pl.BlockSpec((1, FFN_HALO, D), lambda b, i, k: (b, jnp.maximum(i * rb - 1, 0), 0)),
                  vec, bvec, bvec,
                  pl.BlockSpec((D, tf), lambda b, i, k: (0, k)),
                  pl.BlockSpec((D, tf), lambda b, i, k: (0, k + nk)),
                  pl.BlockSpec((3, tf), lambda b, i, k: (0, k)),
                  pl.BlockSpec((tf, D), lambda b, i, k: (k, 0)),
                  bvec, vec],
        out_specs=pl.BlockSpec((1, tm, D), lambda b, i, k: (b, i, 0)),
        out_shape=jax.ShapeDtypeStruct((B, S, D), F32),
        scratch_shapes=[pltpu.VMEM((tm + FFN_HALO, D), BF16), pltpu.VMEM((tm, D), F32)],
        compiler_params=_cparams("arbitrary", "arbitrary", "arbitrary"),
        name="ffn_fused",
    )(x, x, gain.reshape(1, D), sc.reshape(B, 1, D), sh.reshape(B, 1, D), w_in_bf16, w_in_bf16, conv_w, w_out_bf16,
      g2.reshape(B, 1, D), gamma.reshape(1, D))


def _rope_table_kernel(ang_ref, cos_ref, sin_ref, *, half):
    ang = ang_ref[0]
    lane = lax.broadcasted_iota(jnp.int32, ang.shape, 1)
    cos_ref[0] = jnp.where(lane < 2 * half, jnp.cos(ang), 1.0)
    sin_ref[0] = jnp.where(lane < half, -jnp.sin(ang), jnp.where(lane < 2 * half, jnp.sin(ang), 0.0))


def rope_tables(pos, half, tm=512):
    B, S = pos.shape
    inv_freq = ROPE_THETA ** (-jnp.arange(half, dtype=F32) / half)
    row = jnp.concatenate([inv_freq, inv_freq, jnp.zeros((LANES - 2 * half,), F32)])
    ang = pos.astype(F32)[:, :, None] * row
    tm = min(tm, S)
    blk = pl.BlockSpec((1, tm, LANES), lambda b, i: (b, i, 0))
    return pl.pallas_call(
        functools.partial(_rope_table_kernel, half=half),
        grid=(B, S // tm),
        in_specs=[blk], out_specs=[blk, blk],
        out_shape=[jax.ShapeDtypeStruct((B, S, LANES), F32)] * 2,
        compiler_params=_cparams("arbitrary", "arbitrary"),
        name="rope_tables",
    )(ang)


def _rope(x, cos, sin, half):
    lane = lax.broadcasted_iota(jnp.int32, x.shape, 1)
    swapped = jnp.where(lane < half, pltpu.roll(x, LANES - half, axis=1), pltpu.roll(x, half, axis=1))
    return x * cos + swapped * sin


def _online_step(s, mask, v, carry):
    m, l, acc = carry
    h, tq, tk = s.shape
    s = jnp.where(mask[None], s, MASKED)
    m_new = jnp.maximum(m, jnp.max(s, axis=-1, keepdims=True))
    p = jnp.where(mask[None], jnp.exp(s - m_new), 0.0)
    alpha = jnp.exp(m - m_new)
    l = alpha * l + jnp.sum(p, axis=-1, keepdims=True)
    pv = jnp.dot(p.reshape(h * tq, tk).astype(BF16), v, preferred_element_type=F32).reshape(h, tq, -1)
    return m_new, l, alpha * acc + pv


MLA_QK = 2 * LANES


def mla_weights(w_uq, w_ukv):
    H = MLA_HEADS
    wq = w_uq.reshape(MLA_Q_RANK, H, MLA_NOPE + MLA_ROPE)
    wq = jnp.pad(wq, ((0, 0), (0, 0), (0, MLA_QK - MLA_NOPE - MLA_ROPE))).reshape(MLA_Q_RANK, H * MLA_QK)
    return wq.astype(BF16), w_ukv.astype(BF16)


def _mla_prep_kernel(c_ref, cos_ref, sin_ref, qn_ref, kn_ref, wq_ref, wkv_ref, q_ref, k_ref, v_ref):
    H = MLA_HEADS
    c = c_ref[0]
    cos, sin = cos_ref[0], sin_ref[0]
    scale = (MLA_NOPE + MLA_ROPE) ** -0.5
    cq = _rms(c[:, :MLA_Q_RANK], qn_ref[...]).astype(BF16)
    ckv = _rms(c[:, MLA_Q_RANK:MLA_Q_RANK + MLA_KV_RANK], kn_ref[...]).astype(BF16)
    q = jnp.dot(cq, wq_ref[...], preferred_element_type=F32)
    kv = jnp.dot(ckv, wkv_ref[...], preferred_element_type=F32)
    kpe = _rope(c[:, MLA_Q_RANK + MLA_KV_RANK:], cos, sin, MLA_ROPE // 2).astype(BF16)
    for h in range(H):
        b = h * MLA_QK
        q_ref[0, :, b:b + LANES] = (q[:, b:b + LANES] * scale).astype(BF16)
        q_ref[0, :, b + LANES:b + MLA_QK] = (_rope(q[:, b + LANES:b + MLA_QK], cos, sin, MLA_ROPE // 2) * scale).astype(BF16)
        k_ref[0, :, b:b + LANES] = kv[:, b:b + LANES].astype(BF16)
        k_ref[0, :, b + LANES:b + MLA_QK] = kpe
        v_ref[0, :, h * MLA_V:(h + 1) * MLA_V] = kv[:, b + LANES:b + MLA_QK].astype(BF16)


def mla_prep(proj, cos, sin, q_norm, kv_norm, wq, wkv, tm=512):
    B, S, _ = proj.shape
    tm = min(tm, S)
    H = MLA_HEADS
    cw = MLA_Q_RANK + MLA_KV_RANK + LANES
    tab = pl.BlockSpec((1, tm, LANES), lambda b, i: (b, i, 0))
    full = lambda a: pl.BlockSpec(a.shape, lambda b, i: (0,) * a.ndim)
    qn = q_norm.reshape(1, -1)
    kn = kv_norm.reshape(1, -1)
    wide = pl.BlockSpec((1, tm, H * MLA_QK), lambda b, i: (b, i, 0))
    return pl.pallas_call(
        _mla_prep_kernel,
        grid=(B, S // tm),
        in_specs=[pl.BlockSpec((1, tm, cw), lambda b, i: (b, i, BLK_MLA)), tab, tab,
                  full(qn), full(kn), full(wq), full(wkv)],
        out_specs=[wide, wide, pl.BlockSpec((1, tm, H * MLA_V), lambda b, i: (b, i, 0))],
        out_shape=[jax.ShapeDtypeStruct((B, S, H * MLA_QK), BF16), jax.ShapeDtypeStruct((B, S, H * MLA_QK), BF16),
                   jax.ShapeDtypeStruct((B, S, H * MLA_V), BF16)],
        compiler_params=_cparams("arbitrary", "arbitrary"),
        name="mla_prep",
    )(proj, cos, sin, qn, kn, wq, wkv)


def _mla_attn_kernel(q_ref, k_ref, v_ref, o_ref, *, tk):
    tq = q_ref.shape[1]
    q0 = pl.program_id(2) * tq
    q = q_ref[0]
    init = (jnp.full((1, tq, 1), MASKED, F32), jnp.zeros((1, tq, 1), F32), jnp.zeros((1, tq, MLA_V), F32))
    everything = jnp.full((tq, tk), True)

    def body(masked, kt, carry):
        k0 = pl.multiple_of(kt * tk, tk)
        s = _dot_t(q, k_ref[0, pl.ds(k0, tk), :])[None]
        if masked:
            mask = (k0 + lax.broadcasted_iota(jnp.int32, (tq, tk), 1)) <= (q0 + lax.broadcasted_iota(jnp.int32, (tq, tk), 0))
        else:
            mask = everything
        return _online_step(s, mask, v_ref[0, pl.ds(k0, tk), :], carry)

    n_full = q0 // tk
    n_all = (q0 + tq + tk - 1) // tk
    carry = lax.fori_loop(0, n_full, functools.partial(body, False), init)
    _, l, acc = lax.fori_loop(n_full, n_all, functools.partial(body, True), carry)
    o_ref[0] = (acc / l)[0]


def mla_attention(q, k, v, tq=512, tk=512):
    B, S, _ = q.shape
    H = MLA_HEADS
    tq, tk = min(tq, S), min(tk, S)
    return pl.pallas_call(
        functools.partial(_mla_attn_kernel, tk=tk),
        grid=(B, H, S // tq),
        in_specs=[pl.BlockSpec((1, tq, MLA_QK), lambda b, h, i: (b, i, h)),
                  pl.BlockSpec((1, S, MLA_QK), lambda b, h, i: (b, 0, h)),
                  pl.BlockSpec((1, S, MLA_V), lambda b, h, i: (b, 0, h))],
        out_specs=pl.BlockSpec((1, tq, MLA_V), lambda b, h, i: (b, i, h)),
        out_shape=jax.ShapeDtypeStruct((B, S, H * MLA_V), F32),
        compiler_params=_cparams("arbitrary", "arbitrary", "arbitrary"),
        name="mla_attention",
    )(q, k, v)


S5_CHUNK = 16
S5_GPB = LANES // S5_GROUP
HI = lax.Precision.HIGHEST


def s5_params(a_re, a_im, b_re, b_im, c_re, c_im, log_step, n_chunks):
    G, P = a_re.shape
    L = S5_CHUNK
    CI = b_re.shape[-1]
    CO = c_re.shape[1]
    NB = G // S5_GPB
    step = jnp.exp(log_step.astype(F32))[:, None]
    are, aim = a_re.astype(F32), a_im.astype(F32)
    mag = jnp.exp(are * step)
    lb_re, lb_im = mag * jnp.cos(aim * step), mag * jnp.sin(aim * step)
    den = are * are + aim * aim
    nr, ni = lb_re - 1.0, lb_im
    g_re = (nr * are + ni * aim) / den
    g_im = (ni * are - nr * aim) / den
    br, bi = b_re.astype(F32), b_im.astype(F32)
    bb_re = g_re[..., None] * br - g_im[..., None] * bi
    bb_im = g_re[..., None] * bi + g_im[..., None] * br

    def lam_pow(d):
        d = jnp.asarray(d, F32)[:, None, None]
        pmag = jnp.exp(are * step * d)
        return pmag * jnp.cos(aim * step * d), pmag * jnp.sin(aim * step * d)

    pw_re, pw_im = lam_pow(np.arange(L + 1))
    m_re = pw_re[..., None] * bb_re - pw_im[..., None] * bb_im
    m_im = pw_re[..., None] * bb_im + pw_im[..., None] * bb_re
    cr, ci_ = c_re.astype(F32), c_im.astype(F32)
    kd = (jnp.einsum('gop,dgpi->dgoi', cr, m_re[:L], precision=HI)
          - jnp.einsum('gop,dgpi->dgoi', ci_, m_im[:L], precision=HI))
    s_idx = np.arange(L)[:, None]
    t_idx = np.arange(L)[None, :]
    dmat = np.clip(t_idx - s_idx, 0, L - 1)
    eye = jnp.eye(S5_GPB, dtype=F32)
    kbd = jnp.einsum('dbgoi,gh->bdgiho', kd.reshape(L, NB, S5_GPB, CO, CI), eye).reshape(NB, L, LANES, LANES)
    tb = jnp.where((t_idx >= s_idx)[None, :, :, None, None], kbd[:, dmat], 0.0)
    T = tb.transpose(0, 1, 3, 2, 4).reshape(NB, L * LANES, L * LANES).astype(BF16)

    def to_p(m):
        m = m[::-1].reshape(L, NB, S5_GPB, P, CI)
        return jnp.einsum('sbgpi,gh->bsgihp', m, eye).astype(BF16).reshape(NB, L * LANES, S5_GPB * P)

    q_re = cr[None] * pw_re[1:, :, None, :] - ci_[None] * pw_im[1:, :, None, :]
    q_im = -(cr[None] * pw_im[1:, :, None, :] + ci_[None] * pw_re[1:, :, None, :])

    def to_q(m):
        m = m.reshape(L, NB, S5_GPB, CO, P)
        return jnp.einsum('tbgop,gh->bgptho', m, eye).astype(BF16).reshape(NB, S5_GPB * P, L * LANES)

    levels = max(1, int(np.ceil(np.log2(n_chunks))))
    lv_re, lv_im = lam_pow(L * 2 ** np.arange(levels))
    lam_lv = jnp.concatenate([lv_re.reshape(levels, NB, S5_GPB * P), lv_im.reshape(levels, NB, S5_GPB * P)], axis=0)
    lam_lv = lam_lv.transpose(1, 0, 2)
    return dict(T=T, p_re=to_p(m_re[:L]), p_im=to_p(m_im[:L]), q_re=to_q(q_re), q_im=to_q(q_im), lam=lam_lv)


def _s5_kernel(u_ref, t_ref, pre_ref, pim_ref, qre_ref, qim_ref, lam_ref, y_ref):
    L = S5_CHUNK
    nc = u_ref.shape[1] // L
    u = jnp.concatenate([u_ref[0, pl.ds(s, nc, stride=L), :].astype(BF16) for s in range(L)], axis=1)
    xr = jnp.dot(u, pre_ref[0], preferred_element_type=F32)
    xi = jnp.dot(u, pim_ref[0], preferred_element_type=F32)
    lam = lam_ref[0]
    levels = lam.shape[0] // 2
    row = lax.broadcasted_iota(jnp.int32, xr.shape, 0)
    for k in range(levels):
        sh = 2 ** k
        if sh >= nc:
            break
        lr, li = lam[k:k + 1, :], lam[levels + k:levels + k + 1, :]
        sr = jnp.where(row >= sh, pltpu.roll(xr, sh, axis=0), 0.0)
        si = jnp.where(row >= sh, pltpu.roll(xi, sh, axis=0), 0.0)
        xr, xi = xr + lr * sr - li * si, xi + lr * si + li * sr
    pr = jnp.where(row >= 1, pltpu.roll(xr, 1, axis=0), 0.0).astype(BF16)
    pi = jnp.where(row >= 1, pltpu.roll(xi, 1, axis=0), 0.0).astype(BF16)
    for t in range(0, L, 2):
        rows, cols = (t + 2) * LANES, slice(t * LANES, (t + 2) * LANES)
        y = (jnp.dot(u[:, :rows], t_ref[0, :rows, cols], preferred_element_type=F32)
             + jnp.dot(pr, qre_ref[0, :, cols], preferred_element_type=F32)
             + jnp.dot(pi, qim_ref[0, :, cols], preferred_element_type=F32))
        y_ref[0, pl.ds(t, nc, stride=L), :] = y[:, :LANES]
        y_ref[0, pl.ds(t + 1, nc, stride=L), :] = y[:, LANES:]


def s5_scan(proj, prm):
    B, S, _ = proj.shape
    L = S5_CHUNK
    NB = GROUP_W // LANES
    W = L * LANES
    ns = S5_GPB * S5_STATE
    op = lambda r, c: pl.BlockSpec((1, r, c), lambda n, b: (n, 0, 0))
    return pl.pallas_call(
        _s5_kernel,
        grid=(NB, B),
        in_specs=[pl.BlockSpec((1, S, LANES), lambda n, b: (b, 0, BLK_S5 * NB + n)),
                  op(W, W), op(W, ns), op(W, ns), op(ns, W), op(ns, W), op(prm["lam"].shape[1], ns)],
        out_specs=pl.BlockSpec((1, S, LANES), lambda n, b: (b, 0, n)),
        out_shape=jax.ShapeDtypeStruct((B, S, GROUP_W), F32),
        compiler_params=_cparams("arbitrary", "arbitrary"),
        name="s5_scan",
    )(proj, prm["T"], prm["p_re"], prm["p_im"], prm["q_re"], prm["q_im"], prm["lam"])


def _s5_post_kernel(y_ref, u_ref, d_ref, w_ref, b_ref, o_ref):
    y = jax.nn.gelu(y_ref[0] + d_ref[...] * u_ref[0])
    z = jnp.dot(y.astype(BF16), w_ref[...], preferred_element_type=F32) + b_ref[...]
    o_ref[0] = y * jax.nn.sigmoid(z)


def s5_post(y, proj, d_skip, w_glu_bf16, b_glu, tm=512):
    B, S, CH = y.shape
    tm = min(tm, S)
    blk = pl.BlockSpec((1, tm, CH), lambda b, i: (b, i, 0))
    vec = pl.BlockSpec((1, CH), lambda b, i: (0, 0))
    return pl.pallas_call(
        _s5_post_kernel,
        grid=(B, S // tm),
        in_specs=[blk, pl.BlockSpec((1, tm, CH), lambda b, i: (b, i, BLK_S5)), vec,
                  pl.BlockSpec((CH, CH), lambda b, i: (0, 0)), vec],
        out_specs=blk,
        out_shape=jax.ShapeDtypeStruct((B, S, CH), F32),
        compiler_params=_cparams("arbitrary", "arbitrary"),
        name="s5_post",
    )(y, proj, d_skip.reshape(1, CH), w_glu_bf16, b_glu.reshape(1, CH))


def _nsa_prep_kernel(kv_ref, cos_ref, sin_ref, kc_ref, vc_ref, ks_ref, vs_ref, kw_ref, vw_ref):
    cos, sin = cos_ref[0], sin_ref[0]
    outs = (kc_ref, vc_ref, ks_ref, vs_ref, kw_ref, vw_ref)
    for n, o_ref in enumerate(outs):
        x = kv_ref[0, :, n * NSA_DIM:(n + 1) * NSA_DIM]
        if n % 2 == 0:
            x = _rope(x, cos, sin, NSA_ROT // 2)
        o_ref[0] = x.astype(BF16)


def nsa_prep(proj, cos, sin, tm=512):
    B, S, _ = proj.shape
    tm = min(tm, S)
    one = pl.BlockSpec((1, tm, NSA_DIM), lambda b, i: (b, i, 0))
    return pl.pallas_call(
        _nsa_prep_kernel,
        grid=(B, S // tm),
        in_specs=[pl.BlockSpec((1, tm, 6 * NSA_DIM), lambda b, i: (b, i, BLK_NSA_KV)), one, one],
        out_specs=[one] * 6,
        out_shape=[jax.ShapeDtypeStruct((B, S, NSA_DIM), BF16)] * 6,
        compiler_params=_cparams("arbitrary", "arbitrary"),
        name="nsa_prep",
    )(proj, cos, sin)


def _nsa_compress_kernel(xk_ref, xv_ref, pk_ref, w1k_ref, w2k_ref, pv_ref, w1v_ref, w2v_ref, ok_ref, ov_ref):
    for x_ref, p_ref, w1_ref, w2_ref, o_ref in ((xk_ref, pk_ref, w1k_ref, w2k_ref, ok_ref),
                                                (xv_ref, pv_ref, w1v_ref, w2v_ref, ov_ref)):
        x = x_ref[0]
        half = x.shape[1]
        a = jnp.dot(x, w1_ref[0:half, :], preferred_element_type=F32)
        b = jnp.dot(x, w1_ref[half:, :], preferred_element_type=F32)
        c = jnp.dot(p_ref[...], w1_ref[...], preferred_element_type=F32)[0:1, :]
        pre = a + pltpu.roll(b, x.shape[0] - 1, axis=0) + c
        h = jax.nn.gelu(pre)
        o_ref[0] = jnp.dot(h.astype(BF16), w2_ref[...], preferred_element_type=F32).astype(BF16)


def nsa_compress(kc, vc, pos_k, w1k, w2k, pos_v, w1v, w2v):
    B, S, D = kc.shape
    nseg = S // CMP_STRIDE
    seg_w = CMP_STRIDE * D
    xk = kc.reshape(B, nseg, seg_w)
    xv = vc.reshape(B, nseg, seg_w)

    def posrow(p):
        return jnp.broadcast_to(p.reshape(1, CMP_BLOCK * D), (SUBLANES, CMP_BLOCK * D)).astype(BF16)

    xs = pl.BlockSpec((1, nseg, seg_w), lambda b: (b, 0, 0))
    ps = pl.BlockSpec((SUBLANES, CMP_BLOCK * D), lambda b: (0, 0))
    w1s = pl.BlockSpec((CMP_BLOCK * D, D), lambda b: (0, 0))
    w2s = pl.BlockSpec((D, D), lambda b: (0, 0))
    os_ = pl.BlockSpec((1, nseg, D), lambda b: (b, 0, 0))
    return pl.pallas_call(
        _nsa_compress_kernel,
        grid=(B,),
        in_specs=[xs, xs, ps, w1s, w2s, ps, w1s, w2s],
        out_specs=[os_, os_],
        out_shape=[jax.ShapeDtypeStruct((B, nseg, D), BF16)] * 2,
        compiler_params=_cparams("arbitrary"),
        name="nsa_compress",
    )(xk, xv, posrow(pos_k), w1k.astype(BF16), w2k.astype(BF16), posrow(pos_v), w1v.astype(BF16), w2v.astype(BF16))


def _nsa_attn_kernel(q_ref, gate_ref, cos_ref, sin_ref, kc_ref, vc_ref, ks_ref, vs_ref, kw_ref, vw_ref, o_ref,
                     *, tk_sel):
    H, D = NSA_HEADS, NSA_DIM
    tq = q_ref.shape[1]
    S = ks_ref.shape[1]
    n_cmp = kc_ref.shape[1]
    n_sel = S // SEL_BLOCK
    q0 = pl.program_id(1) * tq
    scale = D ** -0.5
    cos, sin = cos_ref[0], sin_ref[0]
    q = jnp.concatenate(
        [(_rope(q_ref[0, :, h * D:(h + 1) * D], cos, sin, NSA_ROT // 2) * scale).astype(BF16) for h in range(H)],
        axis=0)

    n_c = lax.broadcasted_iota(jnp.int32, (n_cmp, tq), 0)
    t_c = q0 + lax.broadcasted_iota(jnp.int32, (n_cmp, tq), 1)
    valid1 = (t_c >= n_c * CMP_STRIDE + (CMP_BLOCK - 1)) & (n_c < n_cmp - 1)
    valid = jnp.concatenate([valid1] * H, axis=1)
    s = jnp.where(valid, _dot_t(kc_ref[0], q), MASKED)
    e = jnp.exp(s - jnp.max(s, axis=0, keepdims=True))
    p = jnp.where(valid, e / jnp.sum(e, axis=0, keepdims=True), 0.0)
    o_cmp = lax.dot_general(p.astype(BF16), vc_ref[0], (((0,), (0,)), ((), ())),
                            preferred_element_type=F32).reshape(H, tq, D)

    psum = p[:, 0:tq]
    for h in range(1, H):
        psum = psum + p[:, h * tq:(h + 1) * tq]
    j_o = lax.broadcasted_iota(jnp.int32, (n_sel, n_cmp), 0) * SEL_BLOCK
    n_o = lax.broadcasted_iota(jnp.int32, (n_sel, n_cmp), 1) * CMP_STRIDE
    overlap = jnp.where((n_o < j_o + SEL_BLOCK) & (n_o + CMP_BLOCK > j_o), 1.0, 0.0).astype(BF16)
    p_hi = psum.astype(BF16)
    p_lo = (psum - p_hi.astype(F32)).astype(BF16)
    imp = (jnp.dot(overlap, p_hi, preferred_element_type=F32) + jnp.dot(overlap, p_lo, preferred_element_type=F32))
    j_i = lax.broadcasted_iota(jnp.int32, (n_sel, tq), 0)
    t_blk = (q0 + lax.broadcasted_iota(jnp.int32, (n_sel, tq), 1)) // SEL_BLOCK
    forced = (j_i == 0) | (j_i == t_blk) | (j_i == t_blk - 1)
    imp = jnp.where(forced, 1e6, jnp.where(j_i > t_blk, -1e6, imp))
    j_s = j_i.astype(F32)
    sel = jnp.zeros((n_sel, tq), F32)
    for _ in range(min(SEL_TOPK, n_sel)):
        mx = jnp.max(imp, axis=0, keepdims=True)
        first = jnp.min(jnp.where(imp == mx, j_s, float(n_sel)), axis=0, keepdims=True)
        hit = j_s == first
        sel = jnp.where(hit, 1.0, sel)
        imp = jnp.where(hit, -jnp.inf, imp)
    sel_b = sel.astype(BF16)

    init = (jnp.full((H, tq, 1), MASKED, F32), jnp.zeros((H, tq, 1), F32), jnp.zeros((H, tq, D), F32))

    def sel_body(kt, carry):
        k0 = pl.multiple_of(kt * tk_sel, tk_sel)
        kblk = ks_ref[0, pl.ds(k0, tk_sel), :]
        vblk = vs_ref[0, pl.ds(k0, tk_sel), :]
        sc = _dot_t(q, kblk).reshape(H, tq, tk_sel)
        j_e = lax.broadcasted_iota(jnp.int32, (n_sel, tk_sel), 0)
        c_e = (k0 + lax.broadcasted_iota(jnp.int32, (n_sel, tk_sel), 1)) // SEL_BLOCK
        expand = jnp.where(j_e == c_e, 1.0, 0.0).astype(BF16)
        chosen = lax.dot_general(sel_b, expand, (((0,), (0,)), ((), ())), preferred_element_type=F32)
        kpos = k0 + lax.broadcasted_iota(jnp.int32, (tq, tk_sel), 1)
        tpos = q0 + lax.broadcasted_iota(jnp.int32, (tq, tk_sel), 0)
        return _online_step(sc, (chosen > 0.5) & (kpos <= tpos), vblk, carry)

    n_kt = (q0 + tq + tk_sel - 1) // tk_sel
    _, l_s, acc_s = lax.fori_loop(0, n_kt, sel_body, init)
    o_slc = acc_s / l_s

    span = WINDOW + tq
    w0 = pl.multiple_of(jnp.maximum(q0 - WINDOW, 0), tq)
    sw = _dot_t(q, kw_ref[0, pl.ds(w0, span), :]).reshape(H, tq, span)
    rel = (q0 + lax.broadcasted_iota(jnp.int32, (tq, span), 0)) - (w0 + lax.broadcasted_iota(jnp.int32, (tq, span), 1))
    in_win = ((rel >= 0) & (rel < WINDOW))[None]
    sw = jnp.where(in_win, sw, MASKED)
    ew = jnp.where(in_win, jnp.exp(sw - jnp.max(sw, axis=-1, keepdims=True)), 0.0)
    o_win = jnp.dot(ew.reshape(H * tq, span).astype(BF16), vw_ref[0, pl.ds(w0, span), :],
                    preferred_element_type=F32).reshape(H, tq, D) / jnp.sum(ew, axis=-1, keepdims=True)

    gates = jax.nn.sigmoid(gate_ref[0])
    for h in range(H):
        o_ref[0, :, h * D:(h + 1) * D] = (gates[:, h:h + 1] * o_cmp[h] + gates[:, H + h:H + h + 1] * o_slc[h]
                                          + gates[:, 2 * H + h:2 * H + h + 1] * o_win[h])


def nsa_attention(proj, cos, sin, kcmp, vcmp, ks, vs, kw, vw, tq=128):
    B, S, _ = proj.shape
    D = NSA_DIM
    n_cmp = kcmp.shape[1]
    assert S % tq == 0 and WINDOW % tq == 0 and S >= WINDOW + tq
    tk_sel = min(512, S)
    tile = pl.BlockSpec((1, tq, D), lambda b, i: (b, i, 0))
    seq = pl.BlockSpec((1, S, D), lambda b, i: (b, 0, 0))
    cmp_ = pl.BlockSpec((1, n_cmp, D), lambda b, i: (b, 0, 0))
    return pl.pallas_call(
        functools.partial(_nsa_attn_kernel, tk_sel=tk_sel),
        grid=(B, S // tq),
        in_specs=[pl.BlockSpec((1, tq, NSA_HEADS * D), lambda b, i: (b, i, BLK_NSA_Q)),
                  pl.BlockSpec((1, tq, LANES), lambda b, i: (b, i, BLK_SMALL)),
                  tile, tile, cmp_, cmp_, seq, seq, seq, seq],
        out_specs=pl.BlockSpec((1, tq, NSA_HEADS * D), lambda b, i: (b, i, 0)),
        out_shape=jax.ShapeDtypeStruct((B, S, NSA_HEADS * D), F32),
        compiler_params=_cparams("arbitrary", "arbitrary"),
        name="nsa_attention",
    )(proj, proj, cos, sin, kcmp, vcmp, ks, vs, kw, vw)


GDN_W = GDN_HEADS * GDN_DIM


def _causal_conv_silu(x, halo, w_ref, c0):
    taps = w_ref.shape[0]
    row = lax.broadcasted_iota(jnp.int32, x.shape, 0)
    acc = x * w_ref[taps - 1:taps, c0:c0 + x.shape[1]]
    for d in range(1, taps):
        shifted = pltpu.roll(x, d, axis=0)
        for r in range(d):
            shifted = jnp.where(row == r, halo[SUBLANES - d + r:SUBLANES - d + r + 1, :], shifted)
        acc = acc + shifted * w_ref[taps - 1 - d:taps - d, c0:c0 + x.shape[1]]
    return acc * jax.nn.sigmoid(acc)


def _gdn_prep_kernel(q_ref, k_ref, v_ref, hq_ref, hk_ref, hv_ref, sm_ref, w_ref, a_ref, dt_ref,
                     qo_ref, ko_ref, vo_ref, go_ref):
    first = pl.program_id(1) == 0
    for n, (x_ref, h_ref, o_ref) in enumerate(((q_ref, hq_ref, qo_ref), (k_ref, hk_ref, ko_ref), (v_ref, hv_ref, vo_ref))):
        halo = jnp.where(first, 0.0, h_ref[0])
        y = _causal_conv_silu(x_ref[0], halo, w_ref, n * GDN_W)
        for h in range(GDN_HEADS):
            yh = y[:, h * GDN_DIM:(h + 1) * GDN_DIM]
            if n < 2:
                yh = yh * lax.rsqrt(jnp.sum(yh * yh, axis=-1, keepdims=True) + 1e-6)
            if n == 0:
                yh = yh * GDN_DIM ** -0.5
            o_ref[0, :, h * GDN_DIM:(h + 1) * GDN_DIM] = yh
    sm = sm_ref[0]
    lane = lax.broadcasted_iota(jnp.int32, sm.shape, 1)
    g = -a_ref[...] * jax.nn.softplus(sm + dt_ref[...])
    go_ref[0] = jnp.where((lane >= GA_LANE) & (lane < GA_LANE + GDN_HEADS), g, jax.nn.sigmoid(sm))


def gdn_prep(proj, conv_w, a_log, dt_bias, tm=256):
    B, S, _ = proj.shape
    tm = min(tm, S)
    rb = tm // SUBLANES
    a_row = jnp.zeros((1, LANES), F32).at[0, GA_LANE:GA_LANE + GDN_HEADS].set(jnp.exp(a_log.astype(F32)))
    dt_row = jnp.zeros((1, LANES), F32).at[0, GA_LANE:GA_LANE + GDN_HEADS].set(dt_bias.astype(F32))

    def main(n):
        return pl.BlockSpec((1, tm, GDN_W), lambda b, i: (b, i, BLK_GDN_Q + n))

    def halo(n):
        return pl.BlockSpec((1, SUBLANES, GDN_W), lambda b, i: (b, jnp.maximum(i * rb - 1, 0), BLK_GDN_Q + n))

    out = pl.BlockSpec((1, tm, GDN_W), lambda b, i: (b, i, 0))
    sm = pl.BlockSpec((1, tm, LANES), lambda b, i: (b, i, BLK_SMALL))
    row = pl.BlockSpec((1, LANES), lambda b, i: (0, 0))
    return pl.pallas_call(
        _gdn_prep_kernel,
        grid=(B, S // tm),
        in_specs=[main(0), main(1), main(2), halo(0), halo(1), halo(2), sm,
                  pl.BlockSpec(conv_w.shape, lambda b, i: (0, 0)), row, row],
        out_specs=[out, out, out, pl.BlockSpec((1, tm, LANES), lambda b, i: (b, i, 0))],
        out_shape=[jax.ShapeDtypeStruct((B, S, GDN_W), F32)] * 3 + [jax.ShapeDtypeStruct((B, S, LANES), F32)],
        compiler_params=_cparams("arbitrary", "arbitrary"),
        name="gdn_prep",
    )(proj, proj, proj, proj, proj, proj, proj, conv_w, a_row, dt_row)


def _split3(x):
    hi = x.astype(BF16)
    r = x - hi.astype(F32)
    mid = r.astype(BF16)
    return hi, mid, (r - mid.astype(F32)).astype(BF16)


def _bmm(a, b):
    return jnp.einsum('nik,nkj->nij', a.astype(BF16), b.astype(BF16), preferred_element_type=F32)


def _bmm_t(a, b):
    return jnp.einsum('nik,njk->nij', a.astype(BF16), b.astype(BF16), preferred_element_type=F32)


def _gdn_core_kernel(q_ref, k_ref, v_ref, g_ref, grow_ref, z_ref, on_ref, o_ref, st_ref):
    C, D, H = GDN_CHUNK, GDN_DIM, GDN_HEADS
    tc = q_ref.shape[1]
    nc = tc // C

    @pl.when(pl.program_id(1) == 0)
    def _():
        st_ref[...] = jnp.zeros_like(st_ref)

    def per_head(ref):
        x = ref[0]
        return jnp.stack([x[c * C:(c + 1) * C, h * D:(h + 1) * D] for c in range(nc) for h in range(H)], axis=0)

    ri = lax.broadcasted_iota(jnp.int32, (C, C), 0)
    ci = lax.broadcasted_iota(jnp.int32, (C, C), 1)
    incl = (ri >= ci)[None]
    strict = (ri > ci)[None]
    eye = jnp.where(ri == ci, 1.0, 0.0)[None]
    upper_ones = jnp.where(ri <= ci, 1.0, 0.0).astype(BF16)
    rt = lax.broadcasted_iota(jnp.int32, (tc, tc), 0)
    ct = lax.broadcasted_iota(jnp.int32, (tc, tc), 1)
    chunk_lower = jnp.where((rt >= ct) & (rt // C == ct // C), 1.0, 0.0).astype(BF16)

    gates = g_ref[0]
    gc_all = sum(jnp.dot(chunk_lower, p, preferred_element_type=F32) for p in _split3(gates))
    grow = grow_ref[0].reshape(nc * SUBLANES, C)
    gr_all = sum(jnp.dot(p, upper_ones, preferred_element_type=F32) for p in _split3(grow))

    def col(x, lane0):
        return jnp.stack([x[c * C:(c + 1) * C, lane0 + h:lane0 + h + 1] for c in range(nc) for h in range(H)], axis=0)

    gc = col(gc_all, GA_LANE)
    beta = col(gates, GB_LANE)
    gr = jnp.stack([gr_all[c * SUBLANES + h:c * SUBLANES + h + 1, :] for c in range(nc) for h in range(H)], axis=0)
    q, k, v = per_head(q_ref), per_head(k_ref), per_head(v_ref)

    decay = jnp.where(incl, jnp.exp(jnp.where(incl, gc - gr, 0.0)), 0.0)
    kb = k * beta
    egc = jnp.exp(gc)
    low = jnp.where(strict, _bmm_t(kb, k) * decay, 0.0)
    s_ = 1
    inv = None
    while s_ < C:
        pm = ((ri // (2 * s_) == ci // (2 * s_)) & (ri % (2 * s_) >= s_) & (ci % (2 * s_) < s_))[None]
        off = jnp.where(pm, low, 0.0)
        inv = eye - off if inv is None else inv - _bmm(inv, _bmm(off, inv))
        s_ *= 2
    x = _bmm(inv, jnp.concatenate([v * beta, kb * egc], axis=2))
    u, w = x[:, :, :D], x[:, :, D:]
    intra = jnp.where(incl, _bmm_t(q, k) * decay, 0.0)
    g_last = gc[:, C - 1:C, :]
    q_dec = q * egc
    k_dec = k * jnp.exp(g_last - gc)
    e_last = jnp.exp(g_last)

    st = [st_ref[h] for h in range(H)]
    for c in range(nc):
        for h in range(H):
            i = c * H + h
            v_new = u[i] - _dot_b(w[i], st[h])
            o = _dot_b(q_dec[i], st[h]) + _dot_b(intra[i], v_new)
            st[h] = st[h] * e_last[i] + lax.dot_general(
                k_dec[i].astype(BF16), v_new.astype(BF16), (((0,), (0,)), ((), ())), preferred_element_type=F32)
            zh = z_ref[0, c * C:(c + 1) * C, h * D:(h + 1) * D]
            o_ref[0, c * C:(c + 1) * C, h * D:(h + 1) * D] = _rms(o, on_ref[...]) * (zh * jax.nn.sigmoid(zh))
    for h in range(H):
        st_ref[h] = st[h]


def gdn_core(q, k, v, gates, proj, o_norm, tc=256):
    B, S, _ = q.shape
    C = GDN_CHUNK
    tc = min(tc, S)
    g_rows = gates[..., GA_LANE:GA_LANE + GDN_HEADS].reshape(B, S // C, C, GDN_HEADS).transpose(0, 1, 3, 2)
    g_rows = jnp.pad(g_rows, ((0, 0), (0, 0), (0, SUBLANES - GDN_HEADS), (0, 0)))
    blk = pl.BlockSpec((1, tc, GDN_W), lambda b, i: (b, i, 0))
    return pl.pallas_call(
        _gdn_core_kernel,
        grid=(B, S // tc),
        in_specs=[blk, blk, blk,
                  pl.BlockSpec((1, tc, LANES), lambda b, i: (b, i, 0)),
                  pl.BlockSpec((1, tc // C, SUBLANES, C), lambda b, i: (b, i, 0, 0)),
                  pl.BlockSpec((1, tc, GDN_W), lambda b, i: (b, i, BLK_GDN_Z)),
                  pl.BlockSpec((1, GDN_DIM), lambda b, i: (0, 0))],
        out_specs=blk,
        out_shape=jax.ShapeDtypeStruct((B, S, GDN_W), F32),
        scratch_shapes=[pltpu.VMEM((GDN_HEADS, GDN_DIM, GDN_DIM), F32)],
        compiler_params=_cparams("arbitrary", "arbitrary"),
        name="gdn_core",
    )(q, k, v, gates, g_rows, proj, o_norm.reshape(1, GDN_DIM))


def kernel(x, c, positions, w_ada, b_ada, norm_pre_mix, norm_post_mix, norm_pre_ffn, norm_post_ffn, w_in, w_out, gn_mla, gn_s5, gn_nsa, mla_q_norm, mla_w_uq, mla_kv_norm, mla_w_ukv, s5_a_re, s5_a_im, s5_b_re, s5_b_im, s5_c_re, s5_c_im, s5_d, s5_log_step, s5_w_glu, s5_b_glu, nsa_pos_k, nsa_w1_k, nsa_w2_k, nsa_pos_v, nsa_w1_v, nsa_w2_v, gdn_conv_w, gdn_a_log, gdn_dt_bias, gdn_o_norm, ffn_w_in, ffn_conv_w, ffn_w_out):
    depth = w_in.shape[0]
    cos_mla, sin_mla = rope_tables(positions, MLA_ROPE // 2)
    cos_nsa, sin_nsa = rope_tables(positions, NSA_ROT // 2)
    for l in range(depth):
        mods = ada_mods(c, w_ada, b_ada, l)
        sh1, sc1, g1, sh2, sc2, g2 = jnp.split(mods, 6, axis=-1)

        proj = normmod_matmul(x, norm_pre_mix[l], sc1, sh1, permute_w_in(w_in, l), F32, tm=512, tn=PROJ_COLS // 3)

        wq, wkv = mla_weights(mla_w_uq[l], mla_w_ukv[l])
        q_a, k_a, v_a = mla_prep(proj, cos_mla, sin_mla, mla_q_norm[l], mla_kv_norm[l], wq, wkv)
        o_a = mla_attention(q_a, k_a, v_a)

        s5p = s5_params(s5_a_re[l], s5_a_im[l], s5_b_re[l], s5_b_im[l], s5_c_re[l], s5_c_im[l], s5_log_step[l],
                        x.shape[1] // S5_CHUNK)
        y_b = s5_scan(proj, s5p)
        o_b = s5_post(y_b, proj, s5_d[l], s5_w_glu[l].astype(BF16), s5_b_glu[l])

        kc, vc, ks, vs, kw, vw = nsa_prep(proj, cos_nsa, sin_nsa)
        kcmp, vcmp = nsa_compress(kc, vc, nsa_pos_k[l], nsa_w1_k[l], nsa_w2_k[l], nsa_pos_v[l], nsa_w1_v[l], nsa_w2_v[l])
        o_c = nsa_attention(proj, cos_nsa, sin_nsa, kcmp, vcmp, ks, vs, kw, vw)

        q_d, k_d, v_d, gates_d = gdn_prep(proj, gdn_conv_w[l], gdn_a_log[l], gdn_dt_bias[l])
        o_d = gdn_core(q_d, k_d, v_d, gates_d, proj, gdn_o_norm[l])

        gn3 = jnp.stack([gn_mla[l], gn_s5[l], gn_nsa[l]])
        x = mix_out(o_a, o_b, o_c, o_d, gn3, cast_layer(w_out, l), x, g1, norm_post_mix[l], tm=512)

        x = ffn_fused(x, norm_pre_ffn[l], sc2, sh2, cast_layer(ffn_w_in, l), ffn_conv_w[l], cast_layer(ffn_w_out, l),
                      g2, norm_post_ffn[l])
    return x
```

```python
import functools

import jax
import jax.numpy as jnp
import numpy as np
from jax import lax
from jax.experimental import pallas as pl
from jax.experimental.pallas import tpu as pltpu

F32 = jnp.float32
BF16 = jnp.bfloat16

NORM_EPS = 1e-6
ROPE_THETA = 500000.0

MLA_HEADS, MLA_NOPE, MLA_ROPE, MLA_V, MLA_Q_RANK, MLA_KV_RANK = 4, 128, 64, 128, 384, 256
S5_GROUP, S5_STATE = 16, 64
NSA_HEADS, NSA_DIM = 4, 128
NSA_ROT = NSA_DIM // 4
CMP_BLOCK, CMP_STRIDE, SEL_BLOCK, SEL_TOPK, WINDOW = 32, 16, 64, 16, 512
GDN_HEADS, GDN_DIM, GDN_CONV, GDN_CHUNK = 4, 128, 4, 64
GROUP_W = 512

LANES = 128
SUBLANES = 8
VMEM_LIMIT_BYTES = 56 * 1024 * 1024
FFN_HALO = 16
MASKED = -1e30

PROJ_COLS = 4992
BLK_S5, BLK_NSA_Q, BLK_GDN_Q, BLK_GDN_Z = 0, 1, 2, 5
BLK_MLA = 4
BLK_NSA_KV = 5
BLK_SMALL = 36
GA_LANE, GB_LANE = 12, 16


def _cparams(*sem):
    return pltpu.CompilerParams(dimension_semantics=sem, vmem_limit_bytes=VMEM_LIMIT_BYTES)


def _rms(x, g):
    return x * lax.rsqrt(jnp.mean(x * x, axis=-1, keepdims=True) + NORM_EPS) * g


def _dot_t(a, b):
    return lax.dot_general(a, b, (((1,), (1,)), ((), ())), preferred_element_type=F32)


def _dot_b(a, b):
    return jnp.dot(a.astype(BF16), b.astype(BF16), preferred_element_type=F32)


def _cast_kernel(w_ref, o_ref):
    o_ref[...] = w_ref[0].astype(BF16)


def cast_layer(w, l, tr=256):
    _, R, C = w.shape
    tr = min(tr, R)
    tc = next((t for t in (2816, 2048, 1024) if C % t == 0), C)
    return pl.pallas_call(
        _cast_kernel,
        grid=(R // tr, C // tc),
        in_specs=[pl.BlockSpec((1, tr, tc), lambda i, j: (l, i, j))],
        out_specs=pl.BlockSpec((tr, tc), lambda i, j: (i, j)),
        out_shape=jax.ShapeDtypeStruct((R, C), BF16),
        compiler_params=_cparams("arbitrary", "arbitrary"),
        name="cast_layer",
    )(w)


def _w_in_moves():
    sizes = (MLA_Q_RANK, MLA_KV_RANK, MLA_ROPE, GROUP_W, GROUP_W) + (NSA_DIM,) * 6 + (3 * NSA_HEADS,) \
        + (GROUP_W,) * 4 + (GDN_HEADS, GDN_HEADS)
    names = ("cq", "ckv", "kpe", "u_s5", "nq", "kc", "vc", "ks", "vs", "kw", "vw", "ngate", "gq", "gk", "gv", "gz", "ga", "gb")
    src = dict(zip(names, np.concatenate([[0], np.cumsum(sizes)[:-1]]).tolist()))
    width = dict(zip(names, sizes))
    order = ("u_s5", "nq", "gq", "gk", "gv", "gz", "cq", "ckv", "kpe", None, "kc", "vc", "ks", "vs", "kw", "vw", "ngate", "ga", "gb")
    moves, dst = [], 0
    for n in order:
        if n is None:
            dst += LANES - MLA_ROPE
            continue
        moves.append((src[n], dst, width[n]))
        dst += width[n]
    return moves


def _permute_kernel(w_ref, o_ref):
    o_ref[...] = jnp.zeros_like(o_ref)
    for s, d, w in _w_in_moves():
        o_ref[:, d:d + w] = w_ref[0, :, s:s + w].astype(BF16)


def permute_w_in(w_in, l, tr=256):
    _, D, C = w_in.shape
    tr = min(tr, D)
    return pl.pallas_call(
        _permute_kernel,
        grid=(D // tr,),
        in_specs=[pl.BlockSpec((1, tr, C), lambda i: (l, i, 0))],
        out_specs=pl.BlockSpec((tr, PROJ_COLS), lambda i: (i, 0)),
        out_shape=jax.ShapeDtypeStruct((D, PROJ_COLS), BF16),
        compiler_params=_cparams("arbitrary"),
        name="permute_w_in",
    )(w_in)


def _ada_kernel(c_ref, w_ref, b_ref, o_ref):
    cond = jax.nn.silu(c_ref[...]).astype(BF16)
    o_ref[...] = jnp.dot(cond, w_ref[0].astype(BF16), preferred_element_type=F32) + b_ref[0]


def ada_mods(c, w_ada, b_ada, l):
    B, D = c.shape
    L, _, N = w_ada.shape
    c_pad = jnp.zeros((SUBLANES, D), F32).at[:B].set(c)
    tn = 1024
    assert N % tn == 0 and B <= SUBLANES
    out = pl.pallas_call(
        _ada_kernel,
        grid=(N // tn,),
        in_specs=[pl.BlockSpec((SUBLANES, D), lambda j: (0, 0)),
                  pl.BlockSpec((1, D, tn), lambda j: (l, 0, j)),
                  pl.BlockSpec((1, 1, tn), lambda j: (l, 0, j))],
        out_specs=pl.BlockSpec((SUBLANES, tn), lambda j: (0, j)),
        out_shape=jax.ShapeDtypeStruct((SUBLANES, N), F32),
        compiler_params=_cparams("arbitrary"),
        name="ada_mods",
    )(c_pad, w_ada, b_ada.reshape(L, 1, N))
    return out[:B]


def _normmod_matmul_kernel(x_ref, g_ref, sc_ref, sh_ref, w_ref, o_ref, h_ref):
    @pl.when(pl.program_id(2) == 0)
    def _():
        h_ref[...] = (_rms(x_ref[0], g_ref[...]) * (1.0 + sc_ref[0]) + sh_ref[0]).astype(BF16)

    o_ref[0] = jnp.dot(h_ref[...], w_ref[...], preferred_element_type=F32).astype(o_ref.dtype)


def normmod_matmul(x, gain, sc, sh, w_bf16, out_dtype, tm, tn):
    B, S, D = x.shape
    N = w_bf16.shape[1]
    tm = min(tm, S)
    assert S % tm == 0 and N % tn == 0
    return pl.pallas_call(
        _normmod_matmul_kernel,
        grid=(B, S // tm, N // tn),
        in_specs=[pl.BlockSpec((1, tm, D), lambda b, i, j: (b, i, 0)),
                  pl.BlockSpec((1, D), lambda b, i, j: (0, 0)),
                  pl.BlockSpec((1, 1, D), lambda b, i, j: (b, 0, 0)),
                  pl.BlockSpec((1, 1, D), lambda b, i, j: (b, 0, 0)),
                  pl.BlockSpec((D, tn), lambda b, i, j: (0, j))],
        out_specs=pl.BlockSpec((1, tm, tn), lambda b, i, j: (b, i, j)),
        out_shape=jax.ShapeDtypeStruct((B, S, N), out_dtype),
        scratch_shapes=[pltpu.VMEM((tm, D), BF16)],
        compiler_params=_cparams("arbitrary", "arbitrary", "arbitrary"),
        name="normmod_matmul",
    )(x, gain.reshape(1, D), sc.reshape(B, 1, D), sh.reshape(B, 1, D), w_bf16)


def _mix_out_kernel(oa_ref, ob_ref, oc_ref, od_ref, gn_ref, w_ref, x_ref, gate_ref, gamma_ref, o_ref):
    gw = oa_ref.shape[-1]
    parts = (_rms(oa_ref[0], gn_ref[0:1, :]), _rms(ob_ref[0], gn_ref[1:2, :]),
             _rms(oc_ref[0], gn_ref[2:3, :]), od_ref[0])
    y = None
    for p, part in enumerate(parts):
        t = jnp.dot(part.astype(BF16), w_ref[p * gw:(p + 1) * gw, :], preferred_element_type=F32)
        y = t if y is None else y + t
    o_ref[0] = x_ref[0] + gate_ref[0] * _rms(y, gamma_ref[...])


def mix_out(o_a, o_b, o_c, o_d, gn3, w_out_bf16, x, gate, gamma, tm):
    B, S, D = x.shape
    gw = o_a.shape[-1]
    tm = min(tm, S)
    grp = pl.BlockSpec((1, tm, gw), lambda b, i: (b, i, 0))
    return pl.pallas_call(
        _mix_out_kernel,
        grid=(B, S // tm),
        in_specs=[grp, grp, grp, grp,
                  pl.BlockSpec((3, gw), lambda b, i: (0, 0)),
                  pl.BlockSpec((4 * gw, D), lambda b, i: (0, 0)),
                  pl.BlockSpec((1, tm, D), lambda b, i: (b, i, 0)),
                  pl.BlockSpec((1, 1, D), lambda b, i: (b, 0, 0)),
                  pl.BlockSpec((1, D), lambda b, i: (0, 0))],
        out_specs=pl.BlockSpec((1, tm, D), lambda b, i: (b, i, 0)),
        out_shape=jax.ShapeDtypeStruct((B, S, D), F32),
        compiler_params=_cparams("arbitrary", "arbitrary"),
        name="mix_out",
    )(o_a, o_b, o_c, o_d, gn3, w_out_bf16, x, gate.reshape(B, 1, D), gamma.reshape(1, D))


def _ffn_kernel(x_ref, xh_ref, gain_ref, sc_ref, sh_ref, wg_ref, wv_ref, cw_ref, wo_ref, g2_ref, gamma_ref, o_ref,
                h_ref, acc_ref):
    i, k = pl.program_id(1), pl.program_id(2)

    @pl.when(k == 0)
    def _():
        mod = lambda t: (_rms(t, gain_ref[...]) * (1.0 + sc_ref[0]) + sh_ref[0]).astype(BF16)
        h_ref[0:FFN_HALO, :] = mod(xh_ref[0])
        h_ref[FFN_HALO:, :] = mod(x_ref[0])
        acc_ref[...] = jnp.zeros_like(acc_ref)

    gate = jnp.dot(h_ref[...], wg_ref[...], preferred_element_type=F32)
    val = jnp.dot(h_ref[FFN_HALO:, :], wv_ref[...], preferred_element_type=F32)
    row = lax.broadcasted_iota(jnp.int32, gate.shape, 0)
    gate = jnp.where((row >= FFN_HALO) | (i > 0), gate, 0.0)
    prev1 = pltpu.roll(gate, 1, axis=0)[FFN_HALO:, :]
    prev2 = pltpu.roll(gate, 2, axis=0)[FFN_HALO:, :]
    conv = cw_ref[0:1, :] * prev2 + cw_ref[1:2, :] * prev1 + cw_ref[2:3, :] * gate[FFN_HALO:, :]
    a = (jax.nn.gelu(conv) * val).astype(BF16)
    acc_ref[...] += jnp.dot(a, wo_ref[...], preferred_element_type=F32)

    @pl.when(k == pl.num_programs(2) - 1)
    def _():
        o_ref[0] = x_ref[0] + g2_ref[0] * _rms(acc_ref[...], gamma_ref[...])


def ffn_fused(x, gain, sc, sh, w_in_bf16, conv_w, w_out_bf16, g2, gamma, tm=512, tf=512):
    B, S, D = x.shape
    F = w_out_bf16.shape[0]
    tm = min(tm, S)
    nk = F // tf
    assert F % tf == 0 and S % tm == 0 and tm % FFN_HALO == 0
    rb = tm // FFN_HALO
    vec = pl.BlockSpec((1, D), lambda b, i, k: (0, 0))
    bvec = pl.BlockSpec((1, 1, D), lambda b, i, k: (b, 0, 0))
    return pl.pallas_call(
        _ffn_kernel,
        grid=(B, S // tm, nk),
        in_specs=[pl.BlockSpec((1, tm, D), lambda b, i, k: (b, i, 0)),
                  pl.BlockSpec((1, FFN_HALO, D), lambda b, i, k: (b, jnp.maximum(i * rb - 1, 0), 0)),
                  vec, bvec, bvec,
                  pl.BlockSpec((D, tf), lambda b, i, k: (0, k)),
                  pl.BlockSpec((D, tf), lambda b, i, k: (0, k + nk)),
                  pl.BlockSpec((3, tf), lambda b, i, k: (0, k)),
                  pl.BlockSpec((tf, D), lambda b, i, k: (k, 0)),
                  bvec, vec],
        out_specs=pl.BlockSpec((1, tm, D), lambda b, i, k: (b, i, 0)),
        out_shape=jax.ShapeDtypeStruct((B, S, D), F32),
        scratch_shapes=[pltpu.VMEM((tm + FFN_HALO, D), BF16), pltpu.VMEM((tm, D), F32)],
        compiler_params=_cparams("arbitrary", "arbitrary", "arbitrary"),
        name="ffn_fused",
    )(x, x, gain.reshape(1, D), sc.reshape(B, 1, D), sh.reshape(B, 1, D), w_in_bf16, w_in_bf16, conv_w, w_out_bf16,
      g2.reshape(B, 1, D), gamma.reshape(1, D))


def _rope_table_kernel(ang_ref, cos_ref, sin_ref, *, half):
    ang = ang_ref[0]
    lane = lax.broadcasted_iota(jnp.int32, ang.shape, 1)
    cos_ref[0] = jnp.where(lane < 2 * half, jnp.cos(ang), 1.0)
    sin_ref[0] = jnp.where(lane < half, -jnp.sin(ang), jnp.where(lane < 2 * half, jnp.sin(ang), 0.0))


def rope_tables(pos, half, tm=512):
    B, S = pos.shape
    inv_freq = ROPE_THETA ** (-jnp.arange(half, dtype=F32) / half)
    row = jnp.concatenate([inv_freq, inv_freq, jnp.zeros((LANES - 2 * half,), F32)])
    ang = pos.astype(F32)[:, :, None] * row
    tm = min(tm, S)
    blk = pl.BlockSpec((1, tm, LANES), lambda b, i: (b, i, 0))
    return pl.pallas_call(
        functools.partial(_rope_table_kernel, half=half),
        grid=(B, S // tm),
        in_specs=[blk], out_specs=[blk, blk],
        out_shape=[jax.ShapeDtypeStruct((B, S, LANES), F32)] * 2,
        compiler_params=_cparams("arbitrary", "arbitrary"),
        name="rope_tables",
    )(ang)


def _rope(x, cos, sin, half):
    lane = lax.broadcasted_iota(jnp.int32, x.shape, 1)
    swapped = jnp.where(lane < half, pltpu.roll(x, LANES - half, axis=1), pltpu.roll(x, half, axis=1))
    return x * cos + swapped * sin


def _online_step(s, bias, v, carry):
    m, l, acc = carry
    h, tq, tk = s.shape
    if bias is not None:
        s = s + bias[None]
    m_new = jnp.maximum(m, jnp.max(s, axis=-1, keepdims=True))
    p = jnp.exp(s - m_new)
    alpha = jnp.exp(m - m_new)
    l = alpha * l + jnp.sum(p, axis=-1, keepdims=True)
    pv = jnp.dot(p.reshape(h * tq, tk).astype(BF16), v, preferred_element_type=F32).reshape(h, tq, -1)
    return m_new, l, alpha * acc + pv


def _flash_sweep(first, last, tile_fn, carry):
    n_pairs = (last - first) // 2

    def pair(j, c):
        a = tile_fn(first + 2 * j)
        b = tile_fn(first + 2 * j + 1)
        return _online_step(*b, _online_step(*a, c))

    carry = lax.fori_loop(0, n_pairs, pair, carry)
    return lax.fori_loop(first + 2 * n_pairs, last, lambda kt, c: _online_step(*tile_fn(kt), c), carry)


MLA_QK = 2 * LANES


def mla_weights(w_uq, w_ukv):
    H = MLA_HEADS
    wq = w_uq.reshape(MLA_Q_RANK, H, MLA_NOPE + MLA_ROPE)
    wq = jnp.pad(wq, ((0, 0), (0, 0), (0, MLA_QK - MLA_NOPE - MLA_ROPE))).reshape(MLA_Q_RANK, H * MLA_QK)
    return wq.astype(BF16), w_ukv.astype(BF16)


def _mla_prep_kernel(c_ref, cos_ref, sin_ref, qn_ref, kn_ref, wq_ref, wkv_ref, q_ref, k_ref, v_ref):
    H = MLA_HEADS
    c = c_ref[0]
    cos, sin = cos_ref[0], sin_ref[0]
    scale = (MLA_NOPE + MLA_ROPE) ** -0.5
    cq = _rms(c[:, :MLA_Q_RANK], qn_ref[...]).astype(BF16)
    ckv = _rms(c[:, MLA_Q_RANK:MLA_Q_RANK + MLA_KV_RANK], kn_ref[...]).astype(BF16)
    q = jnp.dot(cq, wq_ref[...], preferred_element_type=F32)
    kv = jnp.dot(ckv, wkv_ref[...], preferred_element_type=F32)
    kpe = _rope(c[:, MLA_Q_RANK + MLA_KV_RANK:], cos, sin, MLA_ROPE // 2).astype(BF16)
    for h in range(H):
        b = h * MLA_QK
        q_ref[0, :, b:b + LANES] = (q[:, b:b + LANES] * scale).astype(BF16)
        q_ref[0, :, b + LANES:b + MLA_QK] = (_rope(q[:, b + LANES:b + MLA_QK], cos, sin, MLA_ROPE // 2) * scale).astype(BF16)
        k_ref[0, :, b:b + LANES] = kv[:, b:b + LANES].astype(BF16)
        k_ref[0, :, b + LANES:b + MLA_QK] = kpe
        v_ref[0, :, h * MLA_V:(h + 1) * MLA_V] = kv[:, b + LANES:b + MLA_QK].astype(BF16)


def mla_prep(proj, cos, sin, q_norm, kv_norm, wq, wkv, tm=512):
    B, S, _ = proj.shape
    tm = min(tm, S)
    H = MLA_HEADS
    cw = MLA_Q_RANK + MLA_KV_RANK + LANES
    tab = pl.BlockSpec((1, tm, LANES), lambda b, i: (b, i, 0))
    full = lambda a: pl.BlockSpec(a.shape, lambda b, i: (0,) * a.ndim)
    qn = q_norm.reshape(1, -1)
    kn = kv_norm.reshape(1, -1)
    wide = pl.BlockSpec((1, tm, H * MLA_QK), lambda b, i: (b, i, 0))
    return pl.pallas_call(
        _mla_prep_kernel,
        grid=(B, S // tm),
        in_specs=[pl.BlockSpec((1, tm, cw), lambda b, i: (b, i, BLK_MLA)), tab, tab,
                  full(qn), full(kn), full(wq), full(wkv)],
        out_specs=[wide, wide, pl.BlockSpec((1, tm, H * MLA_V), lambda b, i: (b, i, 0))],
        out_shape=[jax.ShapeDtypeStruct((B, S, H * MLA_QK), BF16), jax.ShapeDtypeStruct((B, S, H * MLA_QK), BF16),
                   jax.ShapeDtypeStruct((B, S, H * MLA_V), BF16)],
        compiler_params=_cparams("arbitrary", "arbitrary"),
        name="mla_prep",
    )(proj, cos, sin, qn, kn, wq, wkv)


def _mla_attn_kernel(q_ref, k_ref, v_ref, o_ref, *, tk):
    tq = q_ref.shape[1]
    q0 = pl.program_id(2) * tq
    q = q_ref[0]
    init = (jnp.full((1, tq, 1), MASKED, F32), jnp.zeros((1, tq, 1), F32), jnp.zeros((1, tq, MLA_V), F32))

    def tile(masked, kt):
        k0 = pl.multiple_of(kt * tk, tk)
        s = _dot_t(q, k_ref[0, pl.ds(k0, tk), :])[None]
        bias = None
        if masked:
            causal = (k0 + lax.broadcasted_iota(jnp.int32, (tq, tk), 1)) <= (q0 + lax.broadcasted_iota(jnp.int32, (tq, tk), 0))
            bias = jnp.where(causal, 0.0, MASKED)
        return s, bias, v_ref[0, pl.ds(k0, tk), :]

    n_full = q0 // tk
    n_all = (q0 + tq + tk - 1) // tk
    carry = _flash_sweep(0, n_full, functools.partial(tile, False), init)
    _, l, acc = lax.fori_loop(n_full, n_all, lambda kt, c: _online_step(*tile(True, kt), c), carry)
    o_ref[0] = (acc / l)[0]


def mla_attention(q, k, v, tq=512, tk=512):
    B, S, _ = q.shape
    H = MLA_HEADS
    tq, tk = min(tq, S), min(tk, S)
    return pl.pallas_call(
        functools.partial(_mla_attn_kernel, tk=tk),
        grid=(B, H, S // tq),
        in_specs=[pl.BlockSpec((1, tq, MLA_QK), lambda b, h, i: (b, i, h)),
                  pl.BlockSpec((1, S, MLA_QK), lambda b, h, i: (b, 0, h)),
                  pl.BlockSpec((1, S, MLA_V), lambda b, h, i: (b, 0, h))],
        out_specs=pl.BlockSpec((1, tq, MLA_V), lambda b, h, i: (b, i, h)),
        out_shape=jax.ShapeDtypeStruct((B, S, H * MLA_V), F32),
        compiler_params=_cparams("arbitrary", "arbitrary", "arbitrary"),
        name="mla_attention",
    )(q, k, v)


S5_CHUNK = 16
S5_GPB = LANES // S5_GROUP
HI = lax.Precision.HIGHEST


def s5_params(a_re, a_im, b_re, b_im, c_re, c_im, log_step, n_chunks):
    G, P = a_re.shape
    L = S5_CHUNK
    CI = b_re.shape[-1]
    CO = c_re.shape[1]
    NB = G // S5_GPB
    step = jnp.exp(log_step.astype(F32))[:, None]
    are, aim = a_re.astype(F32), a_im.astype(F32)
    mag = jnp.exp(are * step)
    lb_re, lb_im = mag * jnp.cos(aim * step), mag * jnp.sin(aim * step)
    den = are * are + aim * aim
    nr, ni = lb_re - 1.0, lb_im
    g_re = (nr * are + ni * aim) / den
    g_im = (ni * are - nr * aim) / den
    br, bi = b_re.astype(F32), b_im.astype(F32)
    bb_re = g_re[..., None] * br - g_im[..., None] * bi
    bb_im = g_re[..., None] * bi + g_im[..., None] * br

    def lam_pow(d):
        d = jnp.asarray(d, F32)[:, None, None]
        pmag = jnp.exp(are * step * d)
        return pmag * jnp.cos(aim * step * d), pmag * jnp.sin(aim * step * d)

    pw_re, pw_im = lam_pow(np.arange(L + 1))
    m_re = pw_re[..., None] * bb_re - pw_im[..., None] * bb_im
    m_im = pw_re[..., None] * bb_im + pw_im[..., None] * bb_re
    cr, ci_ = c_re.astype(F32), c_im.astype(F32)
    kd = (jnp.einsum('gop,dgpi->dgoi', cr, m_re[:L], precision=HI)
          - jnp.einsum('gop,dgpi->dgoi', ci_, m_im[:L], precision=HI))
    s_idx = np.arange(L)[:, None]
    t_idx = np.arange(L)[None, :]
    dmat = np.clip(t_idx - s_idx, 0, L - 1)
    eye = jnp.eye(S5_GPB, dtype=F32)
    kbd = jnp.einsum('dbgoi,gh->bdgiho', kd.reshape(L, NB, S5_GPB, CO, CI), eye).reshape(NB, L, LANES, LANES)
    tb = jnp.where((t_idx >= s_idx)[None, :, :, None, None], kbd[:, dmat], 0.0)
    T = tb.transpose(0, 1, 3, 2, 4).reshape(NB, L * LANES, L * LANES).astype(BF16)

    def to_p(m):
        m = m[::-1].reshape(L, NB, S5_GPB, P, CI)
        return jnp.einsum('sbgpi,gh->bsgihp', m, eye).astype(BF16).reshape(NB, L * LANES, S5_GPB * P)

    q_re = cr[None] * pw_re[1:, :, None, :] - ci_[None] * pw_im[1:, :, None, :]
    q_im = -(cr[None] * pw_im[1:, :, None, :] + ci_[None] * pw_re[1:, :, None, :])

    def to_q(m):
        m = m.reshape(L, NB, S5_GPB, CO, P)
        return jnp.einsum('tbgop,gh->bgptho', m, eye).astype(BF16).reshape(NB, S5_GPB * P, L * LANES)

    levels = max(1, int(np.ceil(np.log2(n_chunks))))
    lv_re, lv_im = lam_pow(L * 2 ** np.arange(levels))
    lam_lv = jnp.concatenate([lv_re.reshape(levels, NB, S5_GPB * P), lv_im.reshape(levels, NB, S5_GPB * P)], axis=0)
    lam_lv = lam_lv.transpose(1, 0, 2)
    return dict(T=T, p_re=to_p(m_re[:L]), p_im=to_p(m_im[:L]), q_re=to_q(q_re), q_im=to_q(q_im), lam=lam_lv)


def _s5_kernel(u_ref, t_ref, pre_ref, pim_ref, qre_ref, qim_ref, lam_ref, y_ref):
    L = S5_CHUNK
    nc = u_ref.shape[1] // L
    u = jnp.concatenate([u_ref[0, pl.ds(s, nc, stride=L), :].astype(BF16) for s in range(L)], axis=1)
    xr = jnp.dot(u, pre_ref[0, 0], preferred_element_type=F32)
    xi = jnp.dot(u, pim_ref[0, 0], preferred_element_type=F32)
    lam = lam_ref[0, 0]
    levels = lam.shape[0] // 2
    row = lax.broadcasted_iota(jnp.int32, xr.shape, 0)
    for k in range(levels):
        sh = 2 ** k
        if sh >= nc:
            break
        lr, li = lam[k:k + 1, :], lam[levels + k:levels + k + 1, :]
        sr = jnp.where(row >= sh, pltpu.roll(xr, sh, axis=0), 0.0)
        si = jnp.where(row >= sh, pltpu.roll(xi, sh, axis=0), 0.0)
        xr, xi = xr + lr * sr - li * si, xi + lr * si + li * sr
    pr = jnp.where(row >= 1, pltpu.roll(xr, 1, axis=0), 0.0).astype(BF16)
    pi = jnp.where(row >= 1, pltpu.roll(xi, 1, axis=0), 0.0).astype(BF16)
    for t in range(0, L, 2):
        rows, cols = (t + 2) * LANES, slice(t * LANES, (t + 2) * LANES)
        y = (jnp.dot(u[:, :rows], t_ref[0, 0, :rows, cols], preferred_element_type=F32)
             + jnp.dot(pr, qre_ref[0, 0, :, cols], preferred_element_type=F32)
             + jnp.dot(pi, qim_ref[0, 0, :, cols], preferred_element_type=F32))
        y_ref[0, pl.ds(t, nc, stride=L), :] = y[:, :LANES]
        y_ref[0, pl.ds(t + 1, nc, stride=L), :] = y[:, LANES:]


def s5_scan(proj, prm, l):
    B, S, _ = proj.shape
    L = S5_CHUNK
    NB = GROUP_W // LANES
    W = L * LANES
    ns = S5_GPB * S5_STATE
    op = lambda r, c: pl.BlockSpec((1, 1, r, c), lambda n, b: (l, n, 0, 0))
    return pl.pallas_call(
        _s5_kernel,
        grid=(NB, B),
        in_specs=[pl.BlockSpec((1, S, LANES), lambda n, b: (b, 0, BLK_S5 * NB + n)),
                  op(W, W), op(W, ns), op(W, ns), op(ns, W), op(ns, W), op(prm["lam"].shape[2], ns)],
        out_specs=pl.BlockSpec((1, S, LANES), lambda n, b: (b, 0, n)),
        out_shape=jax.ShapeDtypeStruct((B, S, GROUP_W), F32),
        compiler_params=_cparams("arbitrary", "arbitrary"),
        name="s5_scan",
    )(proj, prm["T"], prm["p_re"], prm["p_im"], prm["q_re"], prm["q_im"], prm["lam"])


def _s5_post_kernel(y_ref, u_ref, d_ref, w_ref, b_ref, o_ref):
    y = jax.nn.gelu(y_ref[0] + d_ref[...] * u_ref[0])
    z = jnp.dot(y.astype(BF16), w_ref[...], preferred_element_type=F32) + b_ref[...]
    o_ref[0] = y * jax.nn.sigmoid(z)


def s5_post(y, proj, d_skip, w_glu_bf16, b_glu, tm=512):
    B, S, CH = y.shape
    tm = min(tm, S)
    blk = pl.BlockSpec((1, tm, CH), lambda b, i: (b, i, 0))
    vec = pl.BlockSpec((1, CH), lambda b, i: (0, 0))
    return pl.pallas_call(
        _s5_post_kernel,
        grid=(B, S // tm),
        in_specs=[blk, pl.BlockSpec((1, tm, CH), lambda b, i: (b, i, BLK_S5)), vec,
                  pl.BlockSpec((CH, CH), lambda b, i: (0, 0)), vec],
        out_specs=blk,
        out_shape=jax.ShapeDtypeStruct((B, S, CH), F32),
        compiler_params=_cparams("arbitrary", "arbitrary"),
        name="s5_post",
    )(y, proj, d_skip.reshape(1, CH), w_glu_bf16, b_glu.reshape(1, CH))


def _nsa_prep_kernel(kv_ref, cos_ref, sin_ref, kc_ref, vc_ref, ks_ref, vs_ref, kw_ref, vw_ref):
    cos, sin = cos_ref[0], sin_ref[0]
    outs = (kc_ref, vc_ref, ks_ref, vs_ref, kw_ref, vw_ref)
    for n, o_ref in enumerate(outs):
        x = kv_ref[0, :, n * NSA_DIM:(n + 1) * NSA_DIM]
        if n % 2 == 0:
            x = _rope(x, cos, sin, NSA_ROT // 2)
        o_ref[0] = x.astype(BF16)


def nsa_prep(proj, cos, sin, tm=512):
    B, S, _ = proj.shape
    tm = min(tm, S)
    one = pl.BlockSpec((1, tm, NSA_DIM), lambda b, i: (b, i, 0))
    return pl.pallas_call(
        _nsa_prep_kernel,
        grid=(B, S // tm),
        in_specs=[pl.BlockSpec((1, tm, 6 * NSA_DIM), lambda b, i: (b, i, BLK_NSA_KV)), one, one],
        out_specs=[one] * 6,
        out_shape=[jax.ShapeDtypeStruct((B, S, NSA_DIM), BF16)] * 6,
        compiler_params=_cparams("arbitrary", "arbitrary"),
        name="nsa_prep",
    )(proj, cos, sin)


def _nsa_compress_kernel(xk_ref, xv_ref, pk_ref, w1k_ref, w2k_ref, pv_ref, w1v_ref, w2v_ref, ok_ref, ov_ref):
    for x_ref, p_ref, w1_ref, w2_ref, o_ref in ((xk_ref, pk_ref, w1k_ref, w2k_ref, ok_ref),
                                                (xv_ref, pv_ref, w1v_ref, w2v_ref, ov_ref)):
        x = x_ref[0]
        half = x.shape[1]
        a = jnp.dot(x, w1_ref[0:half, :], preferred_element_type=F32)
        b = jnp.dot(x, w1_ref[half:, :], preferred_element_type=F32)
        c = jnp.dot(p_ref[...], w1_ref[...], preferred_element_type=F32)[0:1, :]
        pre = a + pltpu.roll(b, x.shape[0] - 1, axis=0) + c
        h = jax.nn.gelu(pre)
        o_ref[0] = jnp.dot(h.astype(BF16), w2_ref[...], preferred_element_type=F32).astype(BF16)


def nsa_compress(kc, vc, pos_k, w1k, w2k, pos_v, w1v, w2v):
    B, S, D = kc.shape
    nseg = S // CMP_STRIDE
    seg_w = CMP_STRIDE * D
    xk = kc.reshape(B, nseg, seg_w)
    xv = vc.reshape(B, nseg, seg_w)

    def posrow(p):
        return jnp.broadcast_to(p.reshape(1, CMP_BLOCK * D), (SUBLANES, CMP_BLOCK * D)).astype(BF16)

    xs = pl.BlockSpec((1, nseg, seg_w), lambda b: (b, 0, 0))
    ps = pl.BlockSpec((SUBLANES, CMP_BLOCK * D), lambda b: (0, 0))
    w1s = pl.BlockSpec((CMP_BLOCK * D, D), lambda b: (0, 0))
    w2s = pl.BlockSpec((D, D), lambda b: (0, 0))
    os_ = pl.BlockSpec((1, nseg, D), lambda b: (b, 0, 0))
    return pl.pallas_call(
        _nsa_compress_kernel,
        grid=(B,),
        in_specs=[xs, xs, ps, w1s, w2s, ps, w1s, w2s],
        out_specs=[os_, os_],
        out_shape=[jax.ShapeDtypeStruct((B, nseg, D), BF16)] * 2,
        compiler_params=_cparams("arbitrary"),
        name="nsa_compress",
    )(xk, xv, posrow(pos_k), w1k.astype(BF16), w2k.astype(BF16), posrow(pos_v), w1v.astype(BF16), w2v.astype(BF16))


def _nsa_attn_kernel(q_ref, gate_ref, cos_ref, sin_ref, kc_ref, vc_ref, ks_ref, vs_ref, kw_ref, vw_ref, o_ref,
                     *, tk_sel):
    H, D = NSA_HEADS, NSA_DIM
    tq = q_ref.shape[1]
    S = ks_ref.shape[1]
    n_cmp = kc_ref.shape[1]
    n_sel = S // SEL_BLOCK
    q0 = pl.program_id(1) * tq
    scale = D ** -0.5
    cos, sin = cos_ref[0], sin_ref[0]
    q = jnp.concatenate(
        [(_rope(q_ref[0, :, h * D:(h + 1) * D], cos, sin, NSA_ROT // 2) * scale).astype(BF16) for h in range(H)],
        axis=0)

    n_c = lax.broadcasted_iota(jnp.int32, (n_cmp, tq), 0)
    t_c = q0 + lax.broadcasted_iota(jnp.int32, (n_cmp, tq), 1)
    valid1 = (t_c >= n_c * CMP_STRIDE + (CMP_BLOCK - 1)) & (n_c < n_cmp - 1)
    valid = jnp.concatenate([valid1] * H, axis=1)
    s = jnp.where(valid, _dot_t(kc_ref[0], q), MASKED)
    e = jnp.exp(s - jnp.max(s, axis=0, keepdims=True))
    p = jnp.where(valid, e / jnp.sum(e, axis=0, keepdims=True), 0.0)
    o_cmp = lax.dot_general(p.astype(BF16), vc_ref[0], (((0,), (0,)), ((), ())),
                            preferred_element_type=F32).reshape(H, tq, D)

    psum = p[:, 0:tq]
    for h in range(1, H):
        psum = psum + p[:, h * tq:(h + 1) * tq]
    j_o = lax.broadcasted_iota(jnp.int32, (n_sel, n_cmp), 0) * SEL_BLOCK
    n_o = lax.broadcasted_iota(jnp.int32, (n_sel, n_cmp), 1) * CMP_STRIDE
    overlap = jnp.where((n_o < j_o + SEL_BLOCK) & (n_o + CMP_BLOCK > j_o), 1.0, 0.0).astype(BF16)
    p_hi = psum.astype(BF16)
    p_lo = (psum - p_hi.astype(F32)).astype(BF16)
    imp = (jnp.dot(overlap, p_hi, preferred_element_type=F32) + jnp.dot(overlap, p_lo, preferred_element_type=F32))
    j_i = lax.broadcasted_iota(jnp.int32, (n_sel, tq), 0)
    t_blk = (q0 + lax.broadcasted_iota(jnp.int32, (n_sel, tq), 1)) // SEL_BLOCK
    forced = (j_i == 0) | (j_i == t_blk) | (j_i == t_blk - 1)
    imp = jnp.where(forced, 1e6, jnp.where(j_i > t_blk, -1e6, imp))
    j_s = j_i.astype(F32)
    sel = jnp.zeros((n_sel, tq), F32)
    for _ in range(min(SEL_TOPK, n_sel)):
        mx = jnp.max(imp, axis=0, keepdims=True)
        first = jnp.min(jnp.where(imp == mx, j_s, float(n_sel)), axis=0, keepdims=True)
        hit = j_s == first
        sel = jnp.where(hit, 1.0, sel)
        imp = jnp.where(hit, -jnp.inf, imp)
    sel_b = sel.astype(BF16)

    init = (jnp.full((H, tq, 1), MASKED, F32), jnp.zeros((H, tq, 1), F32), jnp.zeros((H, tq, D), F32))

    def sel_tile(kt):
        k0 = pl.multiple_of(kt * tk_sel, tk_sel)
        sc = _dot_t(q, ks_ref[0, pl.ds(k0, tk_sel), :]).reshape(H, tq, tk_sel)
        j_e = lax.broadcasted_iota(jnp.int32, (n_sel, tk_sel), 0)
        c_e = (k0 + lax.broadcasted_iota(jnp.int32, (n_sel, tk_sel), 1)) // SEL_BLOCK
        expand = jnp.where(j_e == c_e, 1.0, 0.0).astype(BF16)
        chosen = lax.dot_general(sel_b, expand, (((0,), (0,)), ((), ())), preferred_element_type=F32)
        kpos = k0 + lax.broadcasted_iota(jnp.int32, (tq, tk_sel), 1)
        tpos = q0 + lax.broadcasted_iota(jnp.int32, (tq, tk_sel), 0)
        bias = jnp.where(kpos <= tpos, (chosen - 1.0) * -MASKED, MASKED)
        return sc, bias, vs_ref[0, pl.ds(k0, tk_sel), :]

    n_kt = (q0 + tq + tk_sel - 1) // tk_sel
    _, l_s, acc_s = _flash_sweep(0, n_kt, sel_tile, init)
    o_slc = acc_s / l_s

    span = WINDOW + tq
    w0 = pl.multiple_of(jnp.maximum(q0 - WINDOW, 0), tq)
    sw = _dot_t(q, kw_ref[0, pl.ds(w0, span), :]).reshape(H, tq, span)
    rel = (q0 + lax.broadcasted_iota(jnp.int32, (tq, span), 0)) - (w0 + lax.broadcasted_iota(jnp.int32, (tq, span), 1))
    in_win = ((rel >= 0) & (rel < WINDOW))[None]
    sw = jnp.where(in_win, sw, MASKED)
    ew = jnp.where(in_win, jnp.exp(sw - jnp.max(sw, axis=-1, keepdims=True)), 0.0)
    o_win = jnp.dot(ew.reshape(H * tq, span).astype(BF16), vw_ref[0, pl.ds(w0, span), :],
                    preferred_element_type=F32).reshape(H, tq, D) / jnp.sum(ew, axis=-1, keepdims=True)

    gates = jax.nn.sigmoid(gate_ref[0])
    for h in range(H):
        o_ref[0, :, h * D:(h + 1) * D] = (gates[:, h:h + 1] * o_cmp[h] + gates[:, H + h:H + h + 1] * o_slc[h]
                                          + gates[:, 2 * H + h:2 * H + h + 1] * o_win[h])


def nsa_attention(proj, cos, sin, kcmp, vcmp, ks, vs, kw, vw, tq=128):
    B, S, _ = proj.shape
    D = NSA_DIM
    n_cmp = kcmp.shape[1]
    assert S % tq == 0 and WINDOW % tq == 0 and S >= WINDOW + tq
    tk_sel = min(512, S)
    tile = pl.BlockSpec((1, tq, D), lambda b, i: (b, i, 0))
    seq = pl.BlockSpec((1, S, D), lambda b, i: (b, 0, 0))
    cmp_ = pl.BlockSpec((1, n_cmp, D), lambda b, i: (b, 0, 0))
    return pl.pallas_call(
        functools.partial(_nsa_attn_kernel, tk_sel=tk_sel),
        grid=(B, S // tq),
        in_specs=[pl.BlockSpec((1, tq, NSA_HEADS * D), lambda b, i: (b, i, BLK_NSA_Q)),
                  pl.BlockSpec((1, tq, LANES), lambda b, i: (b, i, BLK_SMALL)),
                  tile, tile, cmp_, cmp_, seq, seq, seq, seq],
        out_specs=pl.BlockSpec((1, tq, NSA_HEADS * D), lambda b, i: (b, i, 0)),
        out_shape=jax.ShapeDtypeStruct((B, S, NSA_HEADS * D), F32),
        compiler_params=_cparams("arbitrary", "arbitrary"),
        name="nsa_attention",
    )(proj, proj, cos, sin, kcmp, vcmp, ks, vs, kw, vw)


GDN_W = GDN_HEADS * GDN_DIM


def _causal_conv_silu(x, halo, w_ref, c0):
    taps = w_ref.shape[0]
    row = lax.broadcasted_iota(jnp.int32, x.shape, 0)
    acc = x * w_ref[taps - 1:taps, c0:c0 + x.shape[1]]
    for d in range(1, taps):
        shifted = pltpu.roll(x, d, axis=0)
        for r in range(d):
            shifted = jnp.where(row == r, halo[SUBLANES - d + r:SUBLANES - d + r + 1, :], shifted)
        acc = acc + shifted * w_ref[taps - 1 - d:taps - d, c0:c0 + x.shape[1]]
    return acc * jax.nn.sigmoid(acc)


def _gdn_prep_kernel(q_ref, k_ref, v_ref, hq_ref, hk_ref, hv_ref, sm_ref, w_ref, a_ref, dt_ref,
                     qo_ref, ko_ref, vo_ref, go_ref):
    first = pl.program_id(1) == 0
    for n, (x_ref, h_ref, o_ref) in enumerate(((q_ref, hq_ref, qo_ref), (k_ref, hk_ref, ko_ref), (v_ref, hv_ref, vo_ref))):
        halo = jnp.where(first, 0.0, h_ref[0])
        y = _causal_conv_silu(x_ref[0], halo, w_ref, n * GDN_W)
        for h in range(GDN_HEADS):
            yh = y[:, h * GDN_DIM:(h + 1) * GDN_DIM]
            if n < 2:
                yh = yh * lax.rsqrt(jnp.sum(yh * yh, axis=-1, keepdims=True) + 1e-6)
            if n == 0:
                yh = yh * GDN_DIM ** -0.5
            o_ref[0, :, h * GDN_DIM:(h + 1) * GDN_DIM] = yh
    sm = sm_ref[0]
    lane = lax.broadcasted_iota(jnp.int32, sm.shape, 1)
    g = -a_ref[...] * jax.nn.softplus(sm + dt_ref[...])
    go_ref[0] = jnp.where((lane >= GA_LANE) & (lane < GA_LANE + GDN_HEADS), g, jax.nn.sigmoid(sm))


def gdn_prep(proj, conv_w, a_log, dt_bias, tm=256):
    B, S, _ = proj.shape
    tm = min(tm, S)
    rb = tm // SUBLANES
    a_row = jnp.zeros((1, LANES), F32).at[0, GA_LANE:GA_LANE + GDN_HEADS].set(jnp.exp(a_log.astype(F32)))
    dt_row = jnp.zeros((1, LANES), F32).at[0, GA_LANE:GA_LANE + GDN_HEADS].set(dt_bias.astype(F32))

    def main(n):
        return pl.BlockSpec((1, tm, GDN_W), lambda b, i: (b, i, BLK_GDN_Q + n))

    def halo(n):
        return pl.BlockSpec((1, SUBLANES, GDN_W), lambda b, i: (b, jnp.maximum(i * rb - 1, 0), BLK_GDN_Q + n))

    out = pl.BlockSpec((1, tm, GDN_W), lambda b, i: (b, i, 0))
    sm = pl.BlockSpec((1, tm, LANES), lambda b, i: (b, i, BLK_SMALL))
    row = pl.BlockSpec((1, LANES), lambda b, i: (0, 0))
    return pl.pallas_call(
        _gdn_prep_kernel,
        grid=(B, S // tm),
        in_specs=[main(0), main(1), main(2), halo(0), halo(1), halo(2), sm,
                  pl.BlockSpec(conv_w.shape, lambda b, i: (0, 0)), row, row],
        out_specs=[out, out, out, pl.BlockSpec((1, tm, LANES), lambda b, i: (b, i, 0))],
        out_shape=[jax.ShapeDtypeStruct((B, S, GDN_W), F32)] * 3 + [jax.ShapeDtypeStruct((B, S, LANES), F32)],
        compiler_params=_cparams("arbitrary", "arbitrary"),
        name="gdn_prep",
    )(proj, proj, proj, proj, proj, proj, proj, conv_w, a_row, dt_row)


def _split3(x):
    hi = x.astype(BF16)
    r = x - hi.astype(F32)
    mid = r.astype(BF16)
    return hi, mid, (r - mid.astype(F32)).astype(BF16)


def _bmm(a, b):
    return jnp.einsum('nik,nkj->nij', a.astype(BF16), b.astype(BF16), preferred_element_type=F32)


def _bmm_t(a, b):
    return jnp.einsum('nik,njk->nij', a.astype(BF16), b.astype(BF16), preferred_element_type=F32)


def _gdn_core_kernel(q_ref, k_ref, v_ref, g_ref, grow_ref, z_ref, on_ref, o_ref, st_ref):
    C, D, H = GDN_CHUNK, GDN_DIM, GDN_HEADS
    tc = q_ref.shape[1]
    nc = tc // C

    @pl.when(pl.program_id(1) == 0)
    def _():
        st_ref[...] = jnp.zeros_like(st_ref)

    def per_head(ref):
        x = ref[0]
        return jnp.stack([x[c * C:(c + 1) * C, h * D:(h + 1) * D] for c in range(nc) for h in range(H)], axis=0)

    ri = lax.broadcasted_iota(jnp.int32, (C, C), 0)
    ci = lax.broadcasted_iota(jnp.int32, (C, C), 1)
    incl = (ri >= ci)[None]
    strict = (ri > ci)[None]
    eye = jnp.where(ri == ci, 1.0, 0.0)[None]
    upper_ones = jnp.where(ri <= ci, 1.0, 0.0).astype(BF16)
    rt = lax.broadcasted_iota(jnp.int32, (tc, tc), 0)
    ct = lax.broadcasted_iota(jnp.int32, (tc, tc), 1)
    chunk_lower = jnp.where((rt >= ct) & (rt // C == ct // C), 1.0, 0.0).astype(BF16)

    gates = g_ref[0]
    gc_all = sum(jnp.dot(chunk_lower, p, preferred_element_type=F32) for p in _split3(gates))
    grow = grow_ref[0].reshape(nc * SUBLANES, C)
    gr_all = sum(jnp.dot(p, upper_ones, preferred_element_type=F32) for p in _split3(grow))

    def col(x, lane0):
        return jnp.stack([x[c * C:(c + 1) * C, lane0 + h:lane0 + h + 1] for c in range(nc) for h in range(H)], axis=0)

    gc = col(gc_all, GA_LANE)
    beta = col(gates, GB_LANE)
    gr = jnp.stack([gr_all[c * SUBLANES + h:c * SUBLANES + h + 1, :] for c in range(nc) for h in range(H)], axis=0)
    q, k, v = per_head(q_ref), per_head(k_ref), per_head(v_ref)

    decay = jnp.where(incl, jnp.exp(jnp.where(incl, gc - gr, 0.0)), 0.0)
    kb = k * beta
    egc = jnp.exp(gc)
    low = jnp.where(strict, _bmm_t(kb, k) * decay, 0.0)
    s_ = 1
    inv = None
    while s_ < C:
        pm = ((ri // (2 * s_) == ci // (2 * s_)) & (ri % (2 * s_) >= s_) & (ci % (2 * s_) < s_))[None]
        off = jnp.where(pm, low, 0.0)
        inv = eye - off if inv is None else inv - _bmm(inv, _bmm(off, inv))
        s_ *= 2
    x = _bmm(inv, jnp.concatenate([v * beta, kb * egc], axis=2))
    u, w = x[:, :, :D], x[:, :, D:]
    intra = jnp.where(incl, _bmm_t(q, k) * decay, 0.0)
    g_last = gc[:, C - 1:C, :]
    q_dec = q * egc
    k_dec = k * jnp.exp(g_last - gc)
    e_last = jnp.exp(g_last)

    st = [st_ref[h] for h in range(H)]
    for c in range(nc):
        for h in range(H):
            i = c * H + h
            v_new = u[i] - _dot_b(w[i], st[h])
            o = _dot_b(q_dec[i], st[h]) + _dot_b(intra[i], v_new)
            st[h] = st[h] * e_last[i] + lax.dot_general(
                k_dec[i].astype(BF16), v_new.astype(BF16), (((0,), (0,)), ((), ())), preferred_element_type=F32)
            zh = z_ref[0, c * C:(c + 1) * C, h * D:(h + 1) * D]
            o_ref[0, c * C:(c + 1) * C, h * D:(h + 1) * D] = _rms(o, on_ref[...]) * (zh * jax.nn.sigmoid(zh))
    for h in range(H):
        st_ref[h] = st[h]


def gdn_core(q, k, v, gates, proj, o_norm, tc=256):
    B, S, _ = q.shape
    C = GDN_CHUNK
    tc = min(tc, S)
    assert S % tc == 0 and tc % C == 0
    g_rows = gates[..., GA_LANE:GA_LANE + GDN_HEADS].reshape(B, S // C, C, GDN_HEADS).transpose(0, 1, 3, 2)
    g_rows = jnp.pad(g_rows, ((0, 0), (0, 0), (0, SUBLANES - GDN_HEADS), (0, 0)))
    blk = pl.BlockSpec((1, tc, GDN_W), lambda b, i: (b, i, 0))
    return pl.pallas_call(
        _gdn_core_kernel,
        grid=(B, S // tc),
        in_specs=[blk, blk, blk,
                  pl.BlockSpec((1, tc, LANES), lambda b, i: (b, i, 0)),
                  pl.BlockSpec((1, tc // C, SUBLANES, C), lambda b, i: (b, i, 0, 0)),
                  pl.BlockSpec((1, tc, GDN_W), lambda b, i: (b, i, BLK_GDN_Z)),
                  pl.BlockSpec((1, GDN_DIM), lambda b, i: (0, 0))],
        out_specs=blk,
        out_shape=jax.ShapeDtypeStruct((B, S, GDN_W), F32),
        scratch_shapes=[pltpu.VMEM((GDN_HEADS, GDN_DIM, GDN_DIM), F32)],
        compiler_params=_cparams("arbitrary", "arbitrary"),
        name="gdn_core",
    )(q, k, v, gates, g_rows, proj, o_norm.reshape(1, GDN_DIM))


def kernel(x, c, positions, w_ada, b_ada, norm_pre_mix, norm_post_mix, norm_pre_ffn, norm_post_ffn, w_in, w_out, gn_mla, gn_s5, gn_nsa, mla_q_norm, mla_w_uq, mla_kv_norm, mla_w_ukv, s5_a_re, s5_a_im, s5_b_re, s5_b_im, s5_c_re, s5_c_im, s5_d, s5_log_step, s5_w_glu, s5_b_glu, nsa_pos_k, nsa_w1_k, nsa_w2_k, nsa_pos_v, nsa_w1_v, nsa_w2_v, gdn_conv_w, gdn_a_log, gdn_dt_bias, gdn_o_norm, ffn_w_in, ffn_conv_w, ffn_w_out):
    depth = w_in.shape[0]
    cos_mla, sin_mla = rope_tables(positions, MLA_ROPE // 2)
    cos_nsa, sin_nsa = rope_tables(positions, NSA_ROT // 2)
    s5p = jax.vmap(functools.partial(s5_params, n_chunks=x.shape[1] // S5_CHUNK))(
        s5_a_re, s5_a_im, s5_b_re, s5_b_im, s5_c_re, s5_c_im, s5_log_step)
    for l in range(depth):
        mods = ada_mods(c, w_ada, b_ada, l)
        sh1, sc1, g1, sh2, sc2, g2 = jnp.split(mods, 6, axis=-1)

        proj = normmod_matmul(x, norm_pre_mix[l], sc1, sh1, permute_w_in(w_in, l), F32, tm=512, tn=PROJ_COLS // 3)

        wq, wkv = mla_weights(mla_w_uq[l], mla_w_ukv[l])
        q_a, k_a, v_a = mla_prep(proj, cos_mla, sin_mla, mla_q_norm[l], mla_kv_norm[l], wq, wkv)
        o_a = mla_attention(q_a, k_a, v_a)

        y_b = s5_scan(proj, s5p, l)
        o_b = s5_post(y_b, proj, s5_d[l], s5_w_glu[l].astype(BF16), s5_b_glu[l])

        kc, vc, ks, vs, kw, vw = nsa_prep(proj, cos_nsa, sin_nsa)
        kcmp, vcmp = nsa_compress(kc, vc, nsa_pos_k[l], nsa_w1_k[l], nsa_w2_k[l], nsa_pos_v[l], nsa_w1_v[l], nsa_w2_v[l])
        o_c = nsa_attention(proj, cos_nsa, sin_nsa, kcmp, vcmp, ks, vs, kw, vw)

        q_d, k_d, v_d, gates_d = gdn_prep(proj, gdn_conv_w[l], gdn_a_log[l], gdn_dt_bias[l])
        o_d = gdn_core(q_d, k_d, v_d, gates_d, proj, gdn_o_norm[l])

        gn3 = jnp.stack([gn_mla[l], gn_s5[l], gn_nsa[l]])
        x = mix_out(o_a, o_b, o_c, o_d, gn3, cast_layer(w_out, l), x, g1, norm_post_mix[l], tm=512)

        x = ffn_fused(x, norm_pre_ffn[l], sc2, sh2, cast_layer(ffn_w_in, l), ffn_conv_w[l], cast_layer(ffn_w_out, l),
                      g2, norm_post_ffn[l])
    return x
```

```python
import functools

import jax
import jax.numpy as jnp
import numpy as np
from jax import lax
from jax.experimental import pallas as pl
from jax.experimental.pallas import tpu as pltpu

F32 = jnp.float32
BF16 = jnp.bfloat16

NORM_EPS = 1e-6
ROPE_THETA = 500000.0

MLA_HEADS, MLA_NOPE, MLA_ROPE, MLA_V, MLA_Q_RANK, MLA_KV_RANK = 4, 128, 64, 128, 384, 256
S5_GROUP, S5_STATE = 16, 64
NSA_HEADS, NSA_DIM = 4, 128
NSA_ROT = NSA_DIM // 4
CMP_BLOCK, CMP_STRIDE, SEL_BLOCK, SEL_TOPK, WINDOW = 32, 16, 64, 16, 512
GDN_HEADS, GDN_DIM, GDN_CONV, GDN_CHUNK = 4, 128, 4, 64
GROUP_W = 512

LANES = 128
SUBLANES = 8
VMEM_LIMIT_BYTES = 56 * 1024 * 1024
FFN_HALO = 16
MASKED = -1e30

PROJ_COLS = 4992
BLK_S5, BLK_NSA_Q, BLK_GDN_Q, BLK_GDN_Z = 0, 1, 2, 5
BLK_MLA = 4
BLK_NSA_KV = 5
BLK_SMALL = 36
GA_LANE, GB_LANE = 12, 16


def _cparams(*sem):
    return pltpu.CompilerParams(dimension_semantics=sem, vmem_limit_bytes=VMEM_LIMIT_BYTES)


def _rms(x, g):
    return x * lax.rsqrt(jnp.mean(x * x, axis=-1, keepdims=True) + NORM_EPS) * g


def _dot_t(a, b):
    return lax.dot_general(a, b, (((1,), (1,)), ((), ())), preferred_element_type=F32)


def _dot_b(a, b):
    return jnp.dot(a.astype(BF16), b.astype(BF16), preferred_element_type=F32)


def _cast_kernel(w_ref, o_ref):
    o_ref[...] = w_ref[0].astype(BF16)


def cast_layer(w, l, tr=256):
    _, R, C = w.shape
    tr = min(tr, R)
    tc = next((t for t in (2816, 2048, 1024) if C % t == 0), C)
    return pl.pallas_call(
        _cast_kernel,
        grid=(R // tr, C // tc),
        in_specs=[pl.BlockSpec((1, tr, tc), lambda i, j: (l, i, j))],
        out_specs=pl.BlockSpec((tr, tc), lambda i, j: (i, j)),
        out_shape=jax.ShapeDtypeStruct((R, C), BF16),
        compiler_params=_cparams("arbitrary", "arbitrary"),
        name="cast_layer",
    )(w)


def _w_in_moves():
    sizes = (MLA_Q_RANK, MLA_KV_RANK, MLA_ROPE, GROUP_W, GROUP_W) + (NSA_DIM,) * 6 + (3 * NSA_HEADS,) \
        + (GROUP_W,) * 4 + (GDN_HEADS, GDN_HEADS)
    names = ("cq", "ckv", "kpe", "u_s5", "nq", "kc", "vc", "ks", "vs", "kw", "vw", "ngate", "gq", "gk", "gv", "gz", "ga", "gb")
    src = dict(zip(names, np.concatenate([[0], np.cumsum(sizes)[:-1]]).tolist()))
    width = dict(zip(names, sizes))
    order = ("u_s5", "nq", "gq", "gk", "gv", "gz", "cq", "ckv", "kpe", None, "kc", "vc", "ks", "vs", "kw", "vw", "ngate", "ga", "gb")
    moves, dst = [], 0
    for n in order:
        if n is None:
            dst += LANES - MLA_ROPE
            continue
        moves.append((src[n], dst, width[n]))
        dst += width[n]
    return moves


def _permute_kernel(w_ref, o_ref):
    o_ref[...] = jnp.zeros_like(o_ref)
    for s, d, w in _w_in_moves():
        o_ref[:, d:d + w] = w_ref[0, :, s:s + w].astype(BF16)


def permute_w_in(w_in, l, tr=256):
    _, D, C = w_in.shape
    tr = min(tr, D)
    return pl.pallas_call(
        _permute_kernel,
        grid=(D // tr,),
        in_specs=[pl.BlockSpec((1, tr, C), lambda i: (l, i, 0))],
        out_specs=pl.BlockSpec((tr, PROJ_COLS), lambda i: (i, 0)),
        out_shape=jax.ShapeDtypeStruct((D, PROJ_COLS), BF16),
        compiler_params=_cparams("arbitrary"),
        name="permute_w_in",
    )(w_in)


def _ada_kernel(c_ref, w_ref, b_ref, o_ref):
    cond = jax.nn.silu(c_ref[...]).astype(BF16)
    o_ref[...] = jnp.dot(cond, w_ref[0].astype(BF16), preferred_element_type=F32) + b_ref[0]


def ada_mods(c, w_ada, b_ada, l):
    B, D = c.shape
    L, _, N = w_ada.shape
    c_pad = jnp.zeros((SUBLANES, D), F32).at[:B].set(c)
    tn = 1024
    assert N % tn == 0 and B <= SUBLANES
    out = pl.pallas_call(
        _ada_kernel,
        grid=(N // tn,),
        in_specs=[pl.BlockSpec((SUBLANES, D), lambda j: (0, 0)),
                  pl.BlockSpec((1, D, tn), lambda j: (l, 0, j)),
                  pl.BlockSpec((1, 1, tn), lambda j: (l, 0, j))],
        out_specs=pl.BlockSpec((SUBLANES, tn), lambda j: (0, j)),
        out_shape=jax.ShapeDtypeStruct((SUBLANES, N), F32),
        compiler_params=_cparams("arbitrary"),
        name="ada_mods",
    )(c_pad, w_ada, b_ada.reshape(L, 1, N))
    return out[:B]


def _normmod_matmul_kernel(x_ref, g_ref, sc_ref, sh_ref, w_ref, o_ref, h_ref):
    @pl.when(pl.program_id(2) == 0)
    def _():
        h_ref[...] = (_rms(x_ref[0], g_ref[...]) * (1.0 + sc_ref[0]) + sh_ref[0]).astype(BF16)

    o_ref[0] = jnp.dot(h_ref[...], w_ref[...], preferred_element_type=F32).astype(o_ref.dtype)


def normmod_matmul(x, gain, sc, sh, w_bf16, out_dtype, tm, tn):
    B, S, D = x.shape
    N = w_bf16.shape[1]
    tm = min(tm, S)
    assert S % tm == 0 and N % tn == 0
    return pl.pallas_call(
        _normmod_matmul_kernel,
        grid=(B, S // tm, N // tn),
        in_specs=[pl.BlockSpec((1, tm, D), lambda b, i, j: (b, i, 0)),
                  pl.BlockSpec((1, D), lambda b, i, j: (0, 0)),
                  pl.BlockSpec((1, 1, D), lambda b, i, j: (b, 0, 0)),
                  pl.BlockSpec((1, 1, D), lambda b, i, j: (b, 0, 0)),
                  pl.BlockSpec((D, tn), lambda b, i, j: (0, j))],
        out_specs=pl.BlockSpec((1, tm, tn), lambda b, i, j: (b, i, j)),
        out_shape=jax.ShapeDtypeStruct((B, S, N), out_dtype),
        scratch_shapes=[pltpu.VMEM((tm, D), BF16)],
        compiler_params=_cparams("arbitrary", "arbitrary", "arbitrary"),
        name="normmod_matmul",
    )(x, gain.reshape(1, D), sc.reshape(B, 1, D), sh.reshape(B, 1, D), w_bf16)


def _mix_out_kernel(oa_ref, ob_ref, oc_ref, od_ref, gn_ref, w_ref, x_ref, gate_ref, gamma_ref, o_ref):
    gw = oa_ref.shape[-1]
    parts = (_rms(oa_ref[0], gn_ref[0:1, :]), _rms(ob_ref[0], gn_ref[1:2, :]),
             _rms(oc_ref[0], gn_ref[2:3, :]), od_ref[0])
    y = None
    for p, part in enumerate(parts):
        t = jnp.dot(part.astype(BF16), w_ref[p * gw:(p + 1) * gw, :], preferred_element_type=F32)
        y = t if y is None else y + t
    o_ref[0] = x_ref[0] + gate_ref[0] * _rms(y, gamma_ref[...])


def mix_out(o_a, o_b, o_c, o_d, gn3, w_out_bf16, x, gate, gamma, tm):
    B, S, D = x.shape
    gw = o_a.shape[-1]
    tm = min(tm, S)
    grp = pl.BlockSpec((1, tm, gw), lambda b, i: (b, i, 0))
    return pl.pallas_call(
        _mix_out_kernel,
        grid=(B, S // tm),
        in_specs=[grp, grp, grp, grp,
                  pl.BlockSpec((3, gw), lambda b, i: (0, 0)),
                  pl.BlockSpec((4 * gw, D), lambda b, i: (0, 0)),
                  pl.BlockSpec((1, tm, D), lambda b, i: (b, i, 0)),
                  pl.BlockSpec((1, 1, D), lambda b, i: (b, 0, 0)),
                  pl.BlockSpec((1, D), lambda b, i: (0, 0))],
        out_specs=pl.BlockSpec((1, tm, D), lambda b, i: (b, i, 0)),
        out_shape=jax.ShapeDtypeStruct((B, S, D), F32),
        compiler_params=_cparams("arbitrary", "arbitrary"),
        name="mix_out",
    )(o_a, o_b, o_c, o_d, gn3, w_out_bf16, x, gate.reshape(B, 1, D), gamma.reshape(1, D))


def _ffn_kernel(x_ref, xh_ref, gain_ref, sc_ref, sh_ref, wg_ref, wv_ref, cw_ref, wo_ref, g2_ref, gamma_ref, o_ref,
                h_ref, acc_ref):
    i, k = pl.program_id(1), pl.program_id(2)

    @pl.when(k == 0)
    def _():
        mod = lambda t: (_rms(t, gain_ref[...]) * (1.0 + sc_ref[0]) + sh_ref[0]).astype(BF16)
        h_ref[0:FFN_HALO, :] = mod(xh_ref[0])
        h_ref[FFN_HALO:, :] = mod(x_ref[0])
        acc_ref[...] = jnp.zeros_like(acc_ref)

    gate = jnp.dot(h_ref[...], wg_ref[...], preferred_element_type=F32)
    val = jnp.dot(h_ref[FFN_HALO:, :], wv_ref[...], preferred_element_type=F32)
    row = lax.broadcasted_iota(jnp.int32, gate.shape, 0)
    gate = jnp.where((row >= FFN_HALO) | (i > 0), gate, 0.0)
    prev1 = pltpu.roll(gate, 1, axis=0)[FFN_HALO:, :]
    prev2 = pltpu.roll(gate, 2, axis=0)[FFN_HALO:, :]
    conv = cw_ref[0:1, :] * prev2 + cw_ref[1:2, :] * prev1 + cw_ref[2:3, :] * gate[FFN_HALO:, :]
    a = (jax.nn.gelu(conv) * val).astype(BF16)
    acc_ref[...] += jnp.dot(a, wo_ref[...], preferred_element_type=F32)

    @pl.when(k == pl.num_programs(2) - 1)
    def _():
        o_ref[0] = x_ref[0] + g2_ref[0] * _rms(acc_ref[...], gamma_ref[...])


def ffn_fused(x, gain, sc, sh, w_in_bf16, conv_w, w_out_bf16, g2, gamma, tm=512, tf=512):
    B, S, D = x.shape
    F = w_out_bf16.shape[0]
    tm = min(tm, S)
    nk = F // tf
    assert F % tf == 0 and S % tm == 0 and tm % FFN_HALO == 0
    rb = tm // FFN_HALO
    vec = pl.BlockSpec((1, D), lambda b, i, k: (0, 0))
    bvec = pl.BlockSpec((1, 1, D), lambda b, i, k: (b, 0, 0))
    return pl.pallas_call(
        _ffn_kernel,
        grid=(B, S // tm, nk),
        in_specs=[pl.BlockSpec((1, tm, D), lambda b, i, k: (b, i, 0)),
                  pl.BlockSpec((1, FFN_HALO, D), lambda b, i, k: (b, jnp.maximum(i * rb - 1, 0), 0)),
                  vec, bvec, bvec,
                  pl.BlockSpec((D, tf), lambda b, i, k: (0, k)),
                  pl.BlockSpec((D, tf), lambda b, i, k: (0, k + nk)),
                  pl.BlockSpec((3, tf), lambda b, i, k: (0, k)),
                  pl.BlockSpec((tf, D), lambda b, i, k: (k, 0)),
                  bvec, vec],
        out_specs=pl.BlockSpec((1, tm, D), lambda b, i, k: (b, i, 0)),
        out_shape=jax.ShapeDtypeStruct((B, S, D), F32),
        scratch_shapes=[pltpu.VMEM((tm + FFN_HALO, D), BF16), pltpu.VMEM((tm, D), F32)],
        compiler_params=_cparams("arbitrary", "arbitrary", "arbitrary"),
        name="ffn_fused",
    )(x, x, gain.reshape(1, D), sc.reshape(B, 1, D), sh.reshape(B, 1, D), w_in_bf16, w_in_bf16, conv_w, w_out_bf16,
      g2.reshape(B, 1, D), gamma.reshape(1, D))


def _rope_table_kernel(ang_ref, cos_ref, sin_ref, *, half):
    ang = ang_ref[0]
    lane = lax.broadcasted_iota(jnp.int32, ang.shape, 1)
    cos_ref[0] = jnp.where(lane < 2 * half, jnp.cos(ang), 1.0)
    sin_ref[0] = jnp.where(lane < half, -jnp.sin(ang), jnp.where(lane < 2 * half, jnp.sin(ang), 0.0))


def rope_tables(pos, half, tm=512):
    B, S = pos.shape
    inv_freq = ROPE_THETA ** (-jnp.arange(half, dtype=F32) / half)
    row = jnp.concatenate([inv_freq, inv_freq, jnp.zeros((LANES - 2 * half,), F32)])
    ang = pos.astype(F32)[:, :, None] * row
    tm = min(tm, S)
    blk = pl.BlockSpec((1, tm, LANES), lambda b, i: (b, i, 0))
    return pl.pallas_call(
        functools.partial(_rope_table_kernel, half=half),
        grid=(B, S // tm),
        in_specs=[blk], out_specs=[blk, blk],
        out_shape=[jax.ShapeDtypeStruct((B, S, LANES), F32)] * 2,
        compiler_params=_cparams("arbitrary", "arbitrary"),
        name="rope_tables",
    )(ang)


def _rope(x, cos, sin, half):
    lane = lax.broadcasted_iota(jnp.int32, x.shape, 1)
    swapped = jnp.where(lane < half, pltpu.roll(x, LANES - half, axis=1), pltpu.roll(x, half, axis=1))
    return x * cos + swapped * sin


def _online_step(s, bias, v, carry):
    m, l, acc = carry
    h, tq, tk = s.shape
    if bias is not None:
        s = s + bias[None]
    m_new = jnp.maximum(m, jnp.max(s, axis=-1, keepdims=True))
    p = jnp.exp(s - m_new)
    alpha = jnp.exp(m - m_new)
    l = alpha * l + jnp.sum(p, axis=-1, keepdims=True)
    pv = jnp.dot(p.reshape(h * tq, tk).astype(BF16), v, preferred_element_type=F32).reshape(h, tq, -1)
    return m_new, l, alpha * acc + pv


def _flash_sweep(first, last, tile_fn, carry):
    n_pairs = (last - first) // 2

    def pair(j, c):
        a = tile_fn(first + 2 * j)
        b = tile_fn(first + 2 * j + 1)
        return _online_step(*b, _online_step(*a, c))

    carry = lax.fori_loop(0, n_pairs, pair, carry)
    return lax.fori_loop(first + 2 * n_pairs, last, lambda kt, c: _online_step(*tile_fn(kt), c), carry)


MLA_QK = 2 * LANES


def mla_weights(w_uq, w_ukv):
    H = MLA_HEADS
    wq = w_uq.reshape(MLA_Q_RANK, H, MLA_NOPE + MLA_ROPE)
    wq = jnp.pad(wq, ((0, 0), (0, 0), (0, MLA_QK - MLA_NOPE - MLA_ROPE))).reshape(MLA_Q_RANK, H * MLA_QK)
    return wq.astype(BF16), w_ukv.astype(BF16)


def _mla_prep_kernel(c_ref, cos_ref, sin_ref, qn_ref, kn_ref, wq_ref, wkv_ref, q_ref, k_ref, v_ref):
    H = MLA_HEADS
    c = c_ref[0]
    cos, sin = cos_ref[0], sin_ref[0]
    scale = (MLA_NOPE + MLA_ROPE) ** -0.5
    cq = _rms(c[:, :MLA_Q_RANK], qn_ref[...]).astype(BF16)
    ckv = _rms(c[:, MLA_Q_RANK:MLA_Q_RANK + MLA_KV_RANK], kn_ref[...]).astype(BF16)
    q = jnp.dot(cq, wq_ref[...], preferred_element_type=F32)
    kv = jnp.dot(ckv, wkv_ref[...], preferred_element_type=F32)
    kpe = _rope(c[:, MLA_Q_RANK + MLA_KV_RANK:], cos, sin, MLA_ROPE // 2).astype(BF16)
    for h in range(H):
        b = h * MLA_QK
        q_ref[0, :, b:b + LANES] = (q[:, b:b + LANES] * scale).astype(BF16)
        q_ref[0, :, b + LANES:b + MLA_QK] = (_rope(q[:, b + LANES:b + MLA_QK], cos, sin, MLA_ROPE // 2) * scale).astype(BF16)
        k_ref[0, :, b:b + LANES] = kv[:, b:b + LANES].astype(BF16)
        k_ref[0, :, b + LANES:b + MLA_QK] = kpe
        v_ref[0, :, h * MLA_V:(h + 1) * MLA_V] = kv[:, b + LANES:b + MLA_QK].astype(BF16)


def mla_prep(proj, cos, sin, q_norm, kv_norm, wq, wkv, tm=512):
    B, S, _ = proj.shape
    tm = min(tm, S)
    H = MLA_HEADS
    cw = MLA_Q_RANK + MLA_KV_RANK + LANES
    tab = pl.BlockSpec((1, tm, LANES), lambda b, i: (b, i, 0))
    full = lambda a: pl.BlockSpec(a.shape, lambda b, i: (0,) * a.ndim)
    qn = q_norm.reshape(1, -1)
    kn = kv_norm.reshape(1, -1)
    wide = pl.BlockSpec((1, tm, H * MLA_QK), lambda b, i: (b, i, 0))
    return pl.pallas_call(
        _mla_prep_kernel,
        grid=(B, S // tm),
        in_specs=[pl.BlockSpec((1, tm, cw), lambda b, i: (b, i, BLK_MLA)), tab, tab,
                  full(qn), full(kn), full(wq), full(wkv)],
        out_specs=[wide, wide, pl.BlockSpec((1, tm, H * MLA_V), lambda b, i: (b, i, 0))],
        out_shape=[jax.ShapeDtypeStruct((B, S, H * MLA_QK), BF16), jax.ShapeDtypeStruct((B, S, H * MLA_QK), BF16),
                   jax.ShapeDtypeStruct((B, S, H * MLA_V), BF16)],
        compiler_params=_cparams("arbitrary", "arbitrary"),
        name="mla_prep",
    )(proj, cos, sin, qn, kn, wq, wkv)


def _mla_attn_kernel(q_ref, k_ref, v_ref, o_ref, *, tk):
    tq = q_ref.shape[1]
    q0 = pl.program_id(2) * tq
    q = q_ref[0]
    init = (jnp.full((1, tq, 1), MASKED, F32), jnp.zeros((1, tq, 1), F32), jnp.zeros((1, tq, MLA_V), F32))

    def tile(masked, kt):
        k0 = pl.multiple_of(kt * tk, tk)
        s = _dot_t(q, k_ref[0, pl.ds(k0, tk), :])[None]
        bias = None
        if masked:
            causal = (k0 + lax.broadcasted_iota(jnp.int32, (tq, tk), 1)) <= (q0 + lax.broadcasted_iota(jnp.int32, (tq, tk), 0))
            bias = jnp.where(causal, 0.0, MASKED)
        return s, bias, v_ref[0, pl.ds(k0, tk), :]

    n_full = q0 // tk
    n_all = (q0 + tq + tk - 1) // tk
    carry = _flash_sweep(0, n_full, functools.partial(tile, False), init)
    _, l, acc = lax.fori_loop(n_full, n_all, lambda kt, c: _online_step(*tile(True, kt), c), carry)
    o_ref[0] = (acc / l)[0]


def mla_attention(q, k, v, tq=512, tk=512):
    B, S, _ = q.shape
    H = MLA_HEADS
    tq, tk = min(tq, S), min(tk, S)
    return pl.pallas_call(
        functools.partial(_mla_attn_kernel, tk=tk),
        grid=(B, H, S // tq),
        in_specs=[pl.BlockSpec((1, tq, MLA_QK), lambda b, h, i: (b, i, h)),
                  pl.BlockSpec((1, S, MLA_QK), lambda b, h, i: (b, 0, h)),
                  pl.BlockSpec((1, S, MLA_V), lambda b, h, i: (b, 0, h))],
        out_specs=pl.BlockSpec((1, tq, MLA_V), lambda b, h, i: (b, i, h)),
        out_shape=jax.ShapeDtypeStruct((B, S, H * MLA_V), F32),
        compiler_params=_cparams("arbitrary", "arbitrary", "arbitrary"),
        name="mla_attention",
    )(q, k, v)


S5_CHUNK = 16
S5_GPB = LANES // S5_GROUP
HI = lax.Precision.HIGHEST


def s5_params(a_re, a_im, b_re, b_im, c_re, c_im, log_step, n_chunks):
    G, P = a_re.shape
    L = S5_CHUNK
    CI = b_re.shape[-1]
    CO = c_re.shape[1]
    NB = G // S5_GPB
    step = jnp.exp(log_step.astype(F32))[:, None]
    are, aim = a_re.astype(F32), a_im.astype(F32)
    mag = jnp.exp(are * step)
    lb_re, lb_im = mag * jnp.cos(aim * step), mag * jnp.sin(aim * step)
    den = are * are + aim * aim
    nr, ni = lb_re - 1.0, lb_im
    g_re = (nr * are + ni * aim) / den
    g_im = (ni * are - nr * aim) / den
    br, bi = b_re.astype(F32), b_im.astype(F32)
    bb_re = g_re[..., None] * br - g_im[..., None] * bi
    bb_im = g_re[..., None] * bi + g_im[..., None] * br

    def lam_pow(d):
        d = jnp.asarray(d, F32)[:, None, None]
        pmag = jnp.exp(are * step * d)
        return pmag * jnp.cos(aim * step * d), pmag * jnp.sin(aim * step * d)

    pw_re, pw_im = lam_pow(np.arange(L + 1))
    m_re = pw_re[..., None] * bb_re - pw_im[..., None] * bb_im
    m_im = pw_re[..., None] * bb_im + pw_im[..., None] * bb_re
    cr, ci_ = c_re.astype(F32), c_im.astype(F32)
    kd = (jnp.einsum('gop,dgpi->dgoi', cr, m_re[:L], precision=HI)
          - jnp.einsum('gop,dgpi->dgoi', ci_, m_im[:L], precision=HI))
    s_idx = np.arange(L)[:, None]
    t_idx = np.arange(L)[None, :]
    dmat = np.clip(t_idx - s_idx, 0, L - 1)
    eye = jnp.eye(S5_GPB, dtype=F32)
    kbd = jnp.einsum('dbgoi,gh->bdgiho', kd.reshape(L, NB, S5_GPB, CO, CI), eye).reshape(NB, L, LANES, LANES)
    tb = jnp.where((t_idx >= s_idx)[None, :, :, None, None], kbd[:, dmat], 0.0)
    T = tb.transpose(0, 1, 3, 2, 4).reshape(NB, L * LANES, L * LANES).astype(BF16)

    def to_p(m):
        m = m[::-1].reshape(L, NB, S5_GPB, P, CI).transpose(1, 0, 4, 2, 3).reshape(NB, L, 1, CI, S5_GPB * P)
        same = (np.arange(S5_GPB)[:, None, None] == (np.arange(S5_GPB * P) // P)[None, None, :])
        return jnp.where(same, m, 0.0).astype(BF16).reshape(NB, L * LANES, S5_GPB * P)

    q_re = cr[None] * pw_re[1:, :, None, :] - ci_[None] * pw_im[1:, :, None, :]
    q_im = -(cr[None] * pw_im[1:, :, None, :] + ci_[None] * pw_re[1:, :, None, :])

    def to_q(m):
        m = m.reshape(L, NB, S5_GPB, CO, P).transpose(1, 2, 4, 0, 3).reshape(NB, S5_GPB, P, L, 1, CO)
        same = (np.arange(S5_GPB)[:, None, None, None, None] == np.arange(S5_GPB)[None, None, None, :, None])
        return jnp.where(same, m, 0.0).astype(BF16).reshape(NB, S5_GPB * P, L * LANES)

    levels = max(1, int(np.ceil(np.log2(n_chunks))))
    lv_re, lv_im = lam_pow(L * 2 ** np.arange(levels))
    lam_lv = jnp.concatenate([lv_re.reshape(levels, NB, S5_GPB * P), lv_im.reshape(levels, NB, S5_GPB * P)], axis=0)
    lam_lv = lam_lv.transpose(1, 0, 2)
    return dict(T=T, p_re=to_p(m_re[:L]), p_im=to_p(m_im[:L]), q_re=to_q(q_re), q_im=to_q(q_im), lam=lam_lv)


def _s5_kernel(u_ref, t_ref, pre_ref, pim_ref, qre_ref, qim_ref, lam_ref, y_ref):
    L = S5_CHUNK
    nc = u_ref.shape[1] // L
    u = jnp.concatenate([u_ref[0, pl.ds(s, nc, stride=L), :].astype(BF16) for s in range(L)], axis=1)
    xr = jnp.dot(u, pre_ref[0, 0], preferred_element_type=F32)
    xi = jnp.dot(u, pim_ref[0, 0], preferred_element_type=F32)
    lam = lam_ref[0, 0]
    levels = lam.shape[0] // 2
    row = lax.broadcasted_iota(jnp.int32, xr.shape, 0)
    for k in range(levels):
        sh = 2 ** k
        if sh >= nc:
            break
        lr, li = lam[k:k + 1, :], lam[levels + k:levels + k + 1, :]
        sr = jnp.where(row >= sh, pltpu.roll(xr, sh, axis=0), 0.0)
        si = jnp.where(row >= sh, pltpu.roll(xi, sh, axis=0), 0.0)
        xr, xi = xr + lr * sr - li * si, xi + lr * si + li * sr
    pr = jnp.where(row >= 1, pltpu.roll(xr, 1, axis=0), 0.0).astype(BF16)
    pi = jnp.where(row >= 1, pltpu.roll(xi, 1, axis=0), 0.0).astype(BF16)
    for t in range(0, L, 2):
        rows, cols = (t + 2) * LANES, slice(t * LANES, (t + 2) * LANES)
        y = (jnp.dot(u[:, :rows], t_ref[0, 0, :rows, cols], preferred_element_type=F32)
             + jnp.dot(pr, qre_ref[0, 0, :, cols], preferred_element_type=F32)
             + jnp.dot(pi, qim_ref[0, 0, :, cols], preferred_element_type=F32))
        y_ref[0, pl.ds(t, nc, stride=L), :] = y[:, :LANES]
        y_ref[0, pl.ds(t + 1, nc, stride=L), :] = y[:, LANES:]


def s5_scan(proj, prm, l):
    B, S, _ = proj.shape
    L = S5_CHUNK
    NB = GROUP_W // LANES
    W = L * LANES
    ns = S5_GPB * S5_STATE
    op = lambda r, c: pl.BlockSpec((1, 1, r, c), lambda n, b: (l, n, 0, 0))
    return pl.pallas_call(
        _s5_kernel,
        grid=(NB, B),
        in_specs=[pl.BlockSpec((1, S, LANES), lambda n, b: (b, 0, BLK_S5 * NB + n)),
                  op(W, W), op(W, ns), op(W, ns), op(ns, W), op(ns, W), op(prm["lam"].shape[2], ns)],
        out_specs=pl.BlockSpec((1, S, LANES), lambda n, b: (b, 0, n)),
        out_shape=jax.ShapeDtypeStruct((B, S, GROUP_W), F32),
        compiler_params=_cparams("arbitrary", "arbitrary"),
        name="s5_scan",
    )(proj, prm["T"], prm["p_re"], prm["p_im"], prm["q_re"], prm["q_im"], prm["lam"])


def _s5_post_kernel(y_ref, u_ref, d_ref, w_ref, b_ref, o_ref):
    y = jax.nn.gelu(y_ref[0] + d_ref[...] * u_ref[0])
    z = jnp.dot(y.astype(BF16), w_ref[...], preferred_element_type=F32) + b_ref[...]
    o_ref[0] = y * jax.nn.sigmoid(z)


def s5_post(y, proj, d_skip, w_glu_bf16, b_glu, tm=512):
    B, S, CH = y.shape
    tm = min(tm, S)
    blk = pl.BlockSpec((1, tm, CH), lambda b, i: (b, i, 0))
    vec = pl.BlockSpec((1, CH), lambda b, i: (0, 0))
    return pl.pallas_call(
        _s5_post_kernel,
        grid=(B, S // tm),
        in_specs=[blk, pl.BlockSpec((1, tm, CH), lambda b, i: (b, i, BLK_S5)), vec,
                  pl.BlockSpec((CH, CH), lambda b, i: (0, 0)), vec],
        out_specs=blk,
        out_shape=jax.ShapeDtypeStruct((B, S, CH), F32),
        compiler_params=_cparams("arbitrary", "arbitrary"),
        name="s5_post",
    )(y, proj, d_skip.reshape(1, CH), w_glu_bf16, b_glu.reshape(1, CH))


def _nsa_prep_kernel(kv_ref, cos_ref, sin_ref, kc_ref, vc_ref, ks_ref, vs_ref, kw_ref, vw_ref):
    cos, sin = cos_ref[0], sin_ref[0]
    outs = (kc_ref, vc_ref, ks_ref, vs_ref, kw_ref, vw_ref)
    for n, o_ref in enumerate(outs):
        x = kv_ref[0, :, n * NSA_DIM:(n + 1) * NSA_DIM]
        if n % 2 == 0:
            x = _rope(x, cos, sin, NSA_ROT // 2)
        o_ref[0] = x.astype(BF16)


def nsa_prep(proj, cos, sin, tm=512):
    B, S, _ = proj.shape
    tm = min(tm, S)
    one = pl.BlockSpec((1, tm, NSA_DIM), lambda b, i: (b, i, 0))
    return pl.pallas_call(
        _nsa_prep_kernel,
        grid=(B, S // tm),
        in_specs=[pl.BlockSpec((1, tm, 6 * NSA_DIM), lambda b, i: (b, i, BLK_NSA_KV)), one, one],
        out_specs=[one] * 6,
        out_shape=[jax.ShapeDtypeStruct((B, S, NSA_DIM), BF16)] * 6,
        compiler_params=_cparams("arbitrary", "arbitrary"),
        name="nsa_prep",
    )(proj, cos, sin)


def _nsa_compress_kernel(xk_ref, xv_ref, pk_ref, w1k_ref, w2k_ref, pv_ref, w1v_ref, w2v_ref, ok_ref, ov_ref):
    for x_ref, p_ref, w1_ref, w2_ref, o_ref in ((xk_ref, pk_ref, w1k_ref, w2k_ref, ok_ref),
                                                (xv_ref, pv_ref, w1v_ref, w2v_ref, ov_ref)):
        x = x_ref[0]
        half = x.shape[1]
        a = jnp.dot(x, w1_ref[0:half, :], preferred_element_type=F32)
        b = jnp.dot(x, w1_ref[half:, :], preferred_element_type=F32)
        c = jnp.dot(p_ref[...], w1_ref[...], preferred_element_type=F32)[0:1, :]
        pre = a + pltpu.roll(b, x.shape[0] - 1, axis=0) + c
        h = jax.nn.gelu(pre)
        o_ref[0] = jnp.dot(h.astype(BF16), w2_ref[...], preferred_element_type=F32).astype(BF16)


def nsa_compress(kc, vc, pos_k, w1k, w2k, pos_v, w1v, w2v):
    B, S, D = kc.shape
    nseg = S // CMP_STRIDE
    seg_w = CMP_STRIDE * D
    xk = kc.reshape(B, nseg, seg_w)
    xv = vc.reshape(B, nseg, seg_w)

    def posrow(p):
        return jnp.broadcast_to(p.reshape(1, CMP_BLOCK * D), (SUBLANES, CMP_BLOCK * D)).astype(BF16)

    xs = pl.BlockSpec((1, nseg, seg_w), lambda b: (b, 0, 0))
    ps = pl.BlockSpec((SUBLANES, CMP_BLOCK * D), lambda b: (0, 0))
    w1s = pl.BlockSpec((CMP_BLOCK * D, D), lambda b: (0, 0))
    w2s = pl.BlockSpec((D, D), lambda b: (0, 0))
    os_ = pl.BlockSpec((1, nseg, D), lambda b: (b, 0, 0))
    return pl.pallas_call(
        _nsa_compress_kernel,
        grid=(B,),
        in_specs=[xs, xs, ps, w1s, w2s, ps, w1s, w2s],
        out_specs=[os_, os_],
        out_shape=[jax.ShapeDtypeStruct((B, nseg, D), BF16)] * 2,
        compiler_params=_cparams("arbitrary"),
        name="nsa_compress",
    )(xk, xv, posrow(pos_k), w1k.astype(BF16), w2k.astype(BF16), posrow(pos_v), w1v.astype(BF16), w2v.astype(BF16))


def _nsa_attn_kernel(q_ref, gate_ref, cos_ref, sin_ref, kc_ref, vc_ref, ks_ref, vs_ref, kw_ref, vw_ref, o_ref,
                     *, tk_sel):
    H, D = NSA_HEADS, NSA_DIM
    tq = q_ref.shape[1]
    S = ks_ref.shape[1]
    n_cmp = kc_ref.shape[1]
    n_sel = S // SEL_BLOCK
    q0 = pl.program_id(1) * tq
    scale = D ** -0.5
    cos, sin = cos_ref[0], sin_ref[0]
    q = jnp.concatenate(
        [(_rope(q_ref[0, :, h * D:(h + 1) * D], cos, sin, NSA_ROT // 2) * scale).astype(BF16) for h in range(H)],
        axis=0)

    n_c = lax.broadcasted_iota(jnp.int32, (n_cmp, tq), 0)
    t_c = q0 + lax.broadcasted_iota(jnp.int32, (n_cmp, tq), 1)
    valid1 = (t_c >= n_c * CMP_STRIDE + (CMP_BLOCK - 1)) & (n_c < n_cmp - 1)
    valid = jnp.concatenate([valid1] * H, axis=1)
    s = jnp.where(valid, _dot_t(kc_ref[0], q), MASKED)
    e = jnp.exp(s - jnp.max(s, axis=0, keepdims=True))
    p = jnp.where(valid, e / jnp.sum(e, axis=0, keepdims=True), 0.0)
    o_cmp = lax.dot_general(p.astype(BF16), vc_ref[0], (((0,), (0,)), ((), ())),
                            preferred_element_type=F32).reshape(H, tq, D)

    psum = p[:, 0:tq]
    for h in range(1, H):
        psum = psum + p[:, h * tq:(h + 1) * tq]
    j_o = lax.broadcasted_iota(jnp.int32, (n_sel, n_cmp), 0) * SEL_BLOCK
    n_o = lax.broadcasted_iota(jnp.int32, (n_sel, n_cmp), 1) * CMP_STRIDE
    overlap = jnp.where((n_o < j_o + SEL_BLOCK) & (n_o + CMP_BLOCK > j_o), 1.0, 0.0).astype(BF16)
    p_hi = psum.astype(BF16)
    p_lo = (psum - p_hi.astype(F32)).astype(BF16)
    imp = (jnp.dot(overlap, p_hi, preferred_element_type=F32) + jnp.dot(overlap, p_lo, preferred_element_type=F32))
    j_i = lax.broadcasted_iota(jnp.int32, (n_sel, tq), 0)
    t_blk = (q0 + lax.broadcasted_iota(jnp.int32, (n_sel, tq), 1)) // SEL_BLOCK
    forced = (j_i == 0) | (j_i == t_blk) | (j_i == t_blk - 1)
    imp = jnp.where(forced, 1e6, jnp.where(j_i > t_blk, -1e6, imp))
    j_s = j_i.astype(F32)
    sel = jnp.zeros((n_sel, tq), F32)
    for _ in range(min(SEL_TOPK, n_sel)):
        mx = jnp.max(imp, axis=0, keepdims=True)
        first = jnp.min(jnp.where(imp == mx, j_s, float(n_sel)), axis=0, keepdims=True)
        hit = j_s == first
        sel = jnp.where(hit, 1.0, sel)
        imp = jnp.where(hit, -jnp.inf, imp)
    sel_b = sel.astype(BF16)

    init = (jnp.full((H, tq, 1), MASKED, F32), jnp.zeros((H, tq, 1), F32), jnp.zeros((H, tq, D), F32))

    def sel_tile(kt):
        k0 = pl.multiple_of(kt * tk_sel, tk_sel)
        sc = _dot_t(q, ks_ref[0, pl.ds(k0, tk_sel), :]).reshape(H, tq, tk_sel)
        j_e = lax.broadcasted_iota(jnp.int32, (n_sel, tk_sel), 0)
        c_e = (k0 + lax.broadcasted_iota(jnp.int32, (n_sel, tk_sel), 1)) // SEL_BLOCK
        expand = jnp.where(j_e == c_e, 1.0, 0.0).astype(BF16)
        chosen = lax.dot_general(sel_b, expand, (((0,), (0,)), ((), ())), preferred_element_type=F32)
        kpos = k0 + lax.broadcasted_iota(jnp.int32, (tq, tk_sel), 1)
        tpos = q0 + lax.broadcasted_iota(jnp.int32, (tq, tk_sel), 0)
        bias = jnp.where(kpos <= tpos, (chosen - 1.0) * -MASKED, MASKED)
        return sc, bias, vs_ref[0, pl.ds(k0, tk_sel), :]

    n_kt = (q0 + tq + tk_sel - 1) // tk_sel
    _, l_s, acc_s = _flash_sweep(0, n_kt, sel_tile, init)
    o_slc = acc_s / l_s

    span = WINDOW + tq
    w0 = pl.multiple_of(jnp.maximum(q0 - WINDOW, 0), tq)
    sw = _dot_t(q, kw_ref[0, pl.ds(w0, span), :]).reshape(H, tq, span)
    rel = (q0 + lax.broadcasted_iota(jnp.int32, (tq, span), 0)) - (w0 + lax.broadcasted_iota(jnp.int32, (tq, span), 1))
    in_win = ((rel >= 0) & (rel < WINDOW))[None]
    sw = jnp.where(in_win, sw, MASKED)
    ew = jnp.where(in_win, jnp.exp(sw - jnp.max(sw, axis=-1, keepdims=True)), 0.0)
    o_win = jnp.dot(ew.reshape(H * tq, span).astype(BF16), vw_ref[0, pl.ds(w0, span), :],
                    preferred_element_type=F32).reshape(H, tq, D) / jnp.sum(ew, axis=-1, keepdims=True)

    gates = jax.nn.sigmoid(gate_ref[0])
    for h in range(H):
        o_ref[0, :, h * D:(h + 1) * D] = (gates[:, h:h + 1] * o_cmp[h] + gates[:, H + h:H + h + 1] * o_slc[h]
                                          + gates[:, 2 * H + h:2 * H + h + 1] * o_win[h])


def nsa_attention(proj, cos, sin, kcmp, vcmp, ks, vs, kw, vw, tq=128):
    B, S, _ = proj.shape
    D = NSA_DIM
    n_cmp = kcmp.shape[1]
    assert S % tq == 0 and WINDOW % tq == 0 and S >= WINDOW + tq
    tk_sel = min(512, S)
    tile = pl.BlockSpec((1, tq, D), lambda b, i: (b, i, 0))
    seq = pl.BlockSpec((1, S, D), lambda b, i: (b, 0, 0))
    cmp_ = pl.BlockSpec((1, n_cmp, D), lambda b, i: (b, 0, 0))
    return pl.pallas_call(
        functools.partial(_nsa_attn_kernel, tk_sel=tk_sel),
        grid=(B, S // tq),
        in_specs=[pl.BlockSpec((1, tq, NSA_HEADS * D), lambda b, i: (b, i, BLK_NSA_Q)),
                  pl.BlockSpec((1, tq, LANES), lambda b, i: (b, i, BLK_SMALL)),
                  tile, tile, cmp_, cmp_, seq, seq, seq, seq],
        out_specs=pl.BlockSpec((1, tq, NSA_HEADS * D), lambda b, i: (b, i, 0)),
        out_shape=jax.ShapeDtypeStruct((B, S, NSA_HEADS * D), F32),
        compiler_params=_cparams("arbitrary", "arbitrary"),
        name="nsa_attention",
    )(proj, proj, cos, sin, kcmp, vcmp, ks, vs, kw, vw)


GDN_W = GDN_HEADS * GDN_DIM


def _causal_conv_silu(x, halo, w_ref, c0):
    taps = w_ref.shape[0]
    row = lax.broadcasted_iota(jnp.int32, x.shape, 0)
    acc = x * w_ref[taps - 1:taps, c0:c0 + x.shape[1]]
    for d in range(1, taps):
        shifted = pltpu.roll(x, d, axis=0)
        for r in range(d):
            shifted = jnp.where(row == r, halo[SUBLANES - d + r:SUBLANES - d + r + 1, :], shifted)
        acc = acc + shifted * w_ref[taps - 1 - d:taps - d, c0:c0 + x.shape[1]]
    return acc * jax.nn.sigmoid(acc)


def _gdn_prep_kernel(q_ref, k_ref, v_ref, hq_ref, hk_ref, hv_ref, sm_ref, w_ref, a_ref, dt_ref,
                     qo_ref, ko_ref, vo_ref, go_ref):
    first = pl.program_id(1) == 0
    for n, (x_ref, h_ref, o_ref) in enumerate(((q_ref, hq_ref, qo_ref), (k_ref, hk_ref, ko_ref), (v_ref, hv_ref, vo_ref))):
        halo = jnp.where(first, 0.0, h_ref[0])
        y = _causal_conv_silu(x_ref[0], halo, w_ref, n * GDN_W)
        for h in range(GDN_HEADS):
            yh = y[:, h * GDN_DIM:(h + 1) * GDN_DIM]
            if n < 2:
                yh = yh * lax.rsqrt(jnp.sum(yh * yh, axis=-1, keepdims=True) + 1e-6)
            if n == 0:
                yh = yh * GDN_DIM ** -0.5
            o_ref[0, :, h * GDN_DIM:(h + 1) * GDN_DIM] = yh
    sm = sm_ref[0]
    lane = lax.broadcasted_iota(jnp.int32, sm.shape, 1)
    g = -a_ref[...] * jax.nn.softplus(sm + dt_ref[...])
    go_ref[0] = jnp.where((lane >= GA_LANE) & (lane < GA_LANE + GDN_HEADS), g, jax.nn.sigmoid(sm))


def gdn_prep(proj, conv_w, a_log, dt_bias, tm=256):
    B, S, _ = proj.shape
    tm = min(tm, S)
    rb = tm // SUBLANES
    a_row = jnp.zeros((1, LANES), F32).at[0, GA_LANE:GA_LANE + GDN_HEADS].set(jnp.exp(a_log.astype(F32)))
    dt_row = jnp.zeros((1, LANES), F32).at[0, GA_LANE:GA_LANE + GDN_HEADS].set(dt_bias.astype(F32))

    def main(n):
        return pl.BlockSpec((1, tm, GDN_W), lambda b, i: (b, i, BLK_GDN_Q + n))

    def halo(n):
        return pl.BlockSpec((1, SUBLANES, GDN_W), lambda b, i: (b, jnp.maximum(i * rb - 1, 0), BLK_GDN_Q + n))

    out = pl.BlockSpec((1, tm, GDN_W), lambda b, i: (b, i, 0))
    sm = pl.BlockSpec((1, tm, LANES), lambda b, i: (b, i, BLK_SMALL))
    row = pl.BlockSpec((1, LANES), lambda b, i: (0, 0))
    return pl.pallas_call(
        _gdn_prep_kernel,
        grid=(B, S // tm),
        in_specs=[main(0), main(1), main(2), halo(0), halo(1), halo(2), sm,
                  pl.BlockSpec(conv_w.shape, lambda b, i: (0, 0)), row, row],
        out_specs=[out, out, out, pl.BlockSpec((1, tm, LANES), lambda b, i: (b, i, 0))],
        out_shape=[jax.ShapeDtypeStruct((B, S, GDN_W), F32)] * 3 + [jax.ShapeDtypeStruct((B, S, LANES), F32)],
        compiler_params=_cparams("arbitrary", "arbitrary"),
        name="gdn_prep",
    )(proj, proj, proj, proj, proj, proj, proj, conv_w, a_row, dt_row)


def _split3(x):
    hi = x.astype(BF16)
    r = x - hi.astype(F32)
    mid = r.astype(BF16)
    return hi, mid, (r - mid.astype(F32)).astype(BF16)


def _bmm(a, b):
    return jnp.einsum('nik,nkj->nij', a.astype(BF16), b.astype(BF16), preferred_element_type=F32)


def _bmm_t(a, b):
    return jnp.einsum('nik,njk->nij', a.astype(BF16), b.astype(BF16), preferred_element_type=F32)


def _gdn_core_kernel(q_ref, k_ref, v_ref, g_ref, grow_ref, z_ref, on_ref, o_ref, st_ref):
    C, D, H = GDN_CHUNK, GDN_DIM, GDN_HEADS
    tc = q_ref.shape[1]
    nc = tc // C

    @pl.when(pl.program_id(1) == 0)
    def _():
        st_ref[...] = jnp.zeros_like(st_ref)

    def per_head(ref):
        x = ref[0]
        return jnp.stack([x[c * C:(c + 1) * C, h * D:(h + 1) * D] for c in range(nc) for h in range(H)], axis=0)

    ri = lax.broadcasted_iota(jnp.int32, (C, C), 0)
    ci = lax.broadcasted_iota(jnp.int32, (C, C), 1)
    incl = (ri >= ci)[None]
    strict = (ri > ci)[None]
    eye = jnp.where(ri == ci, 1.0, 0.0)[None]
    upper_ones = jnp.where(ri <= ci, 1.0, 0.0).astype(BF16)
    rt = lax.broadcasted_iota(jnp.int32, (tc, tc), 0)
    ct = lax.broadcasted_iota(jnp.int32, (tc, tc), 1)
    chunk_lower = jnp.where((rt >= ct) & (rt // C == ct // C), 1.0, 0.0).astype(BF16)

    gates = g_ref[0]
    gc_all = sum(jnp.dot(chunk_lower, p, preferred_element_type=F32) for p in _split3(gates))
    grow = grow_ref[0].reshape(nc * SUBLANES, C)
    gr_all = sum(jnp.dot(p, upper_ones, preferred_element_type=F32) for p in _split3(grow))

    def col(x, lane0):
        return jnp.stack([x[c * C:(c + 1) * C, lane0 + h:lane0 + h + 1] for c in range(nc) for h in range(H)], axis=0)

    gc = col(gc_all, GA_LANE)
    beta = col(gates, GB_LANE)
    gr = jnp.stack([gr_all[c * SUBLANES + h:c * SUBLANES + h + 1, :] for c in range(nc) for h in range(H)], axis=0)
    q, k, v = per_head(q_ref), per_head(k_ref), per_head(v_ref)

    decay = jnp.where(incl, jnp.exp(jnp.where(incl, gc - gr, 0.0)), 0.0)
    kb = k * beta
    egc = jnp.exp(gc)
    low = jnp.where(strict, _bmm_t(kb, k) * decay, 0.0)
    s_ = 1
    inv = None
    while s_ < C:
        pm = ((ri // (2 * s_) == ci // (2 * s_)) & (ri % (2 * s_) >= s_) & (ci % (2 * s_) < s_))[None]
        off = jnp.where(pm, low, 0.0)
        inv = eye - off if inv is None else inv - _bmm(inv, _bmm(off, inv))
        s_ *= 2
    x = _bmm(inv, jnp.concatenate([v * beta, kb * egc], axis=2))
    u, w = x[:, :, :D], x[:, :, D:]
    intra = jnp.where(incl, _bmm_t(q, k) * decay, 0.0)
    g_last = gc[:, C - 1:C, :]
    q_dec = q * egc
    k_dec = k * jnp.exp(g_last - gc)
    e_last = jnp.exp(g_last)

    st = [st_ref[h] for h in range(H)]
    for c in range(nc):
        for h in range(H):
            i = c * H + h
            v_new = u[i] - _dot_b(w[i], st[h])
            o = _dot_b(q_dec[i], st[h]) + _dot_b(intra[i], v_new)
            st[h] = st[h] * e_last[i] + lax.dot_general(
                k_dec[i].astype(BF16), v_new.astype(BF16), (((0,), (0,)), ((), ())), preferred_element_type=F32)
            zh = z_ref[0, c * C:(c + 1) * C, h * D:(h + 1) * D]
            o_ref[0, c * C:(c + 1) * C, h * D:(h + 1) * D] = _rms(o, on_ref[...]) * (zh * jax.nn.sigmoid(zh))
    for h in range(H):
        st_ref[h] = st[h]


def gdn_core(q, k, v, gates, proj, o_norm, tc=512):
    B, S, _ = q.shape
    C = GDN_CHUNK
    tc = min(tc, S)
    assert S % tc == 0 and tc % C == 0
    g_rows = gates[..., GA_LANE:GA_LANE + GDN_HEADS].reshape(B, S // C, C, GDN_HEADS).transpose(0, 1, 3, 2)
    g_rows = jnp.pad(g_rows, ((0, 0), (0, 0), (0, SUBLANES - GDN_HEADS), (0, 0)))
    blk = pl.BlockSpec((1, tc, GDN_W), lambda b, i: (b, i, 0))
    return pl.pallas_call(
        _gdn_core_kernel,
        grid=(B, S // tc),
        in_specs=[blk, blk, blk,
                  pl.BlockSpec((1, tc, LANES), lambda b, i: (b, i, 0)),
                  pl.BlockSpec((1, tc // C, SUBLANES, C), lambda b, i: (b, i, 0, 0)),
                  pl.BlockSpec((1, tc, GDN_W), lambda b, i: (b, i, BLK_GDN_Z)),
                  pl.BlockSpec((1, GDN_DIM), lambda b, i: (0, 0))],
        out_specs=blk,
        out_shape=jax.ShapeDtypeStruct((B, S, GDN_W), F32),
        scratch_shapes=[pltpu.VMEM((GDN_HEADS, GDN_DIM, GDN_DIM), F32)],
        compiler_params=_cparams("arbitrary", "arbitrary"),
        name="gdn_core",
    )(q, k, v, gates, g_rows, proj, o_norm.reshape(1, GDN_DIM))


def kernel(x, c, positions, w_ada, b_ada, norm_pre_mix, norm_post_mix, norm_pre_ffn, norm_post_ffn, w_in, w_out, gn_mla, gn_s5, gn_nsa, mla_q_norm, mla_w_uq, mla_kv_norm, mla_w_ukv, s5_a_re, s5_a_im, s5_b_re, s5_b_im, s5_c_re, s5_c_im, s5_d, s5_log_step, s5_w_glu, s5_b_glu, nsa_pos_k, nsa_w1_k, nsa_w2_k, nsa_pos_v, nsa_w1_v, nsa_w2_v, gdn_conv_w, gdn_a_log, gdn_dt_bias, gdn_o_norm, ffn_w_in, ffn_conv_w, ffn_w_out):
    depth = w_in.shape[0]
    cos_mla, sin_mla = rope_tables(positions, MLA_ROPE // 2)
    cos_nsa, sin_nsa = rope_tables(positions, NSA_ROT // 2)
    s5p = jax.vmap(functools.partial(s5_params, n_chunks=x.shape[1] // S5_CHUNK))(
        s5_a_re, s5_a_im, s5_b_re, s5_b_im, s5_c_re, s5_c_im, s5_log_step)
    for l in range(depth):
        mods = ada_mods(c, w_ada, b_ada, l)
        sh1, sc1, g1, sh2, sc2, g2 = jnp.split(mods, 6, axis=-1)

        proj = normmod_matmul(x, norm_pre_mix[l], sc1, sh1, permute_w_in(w_in, l), F32, tm=512, tn=PROJ_COLS // 3)

        wq, wkv = mla_weights(mla_w_uq[l], mla_w_ukv[l])
        q_a, k_a, v_a = mla_prep(proj, cos_mla, sin_mla, mla_q_norm[l], mla_kv_norm[l], wq, wkv)
        o_a = mla_attention(q_a, k_a, v_a)

        y_b = s5_scan(proj, s5p, l)
        o_b = s5_post(y_b, proj, s5_d[l], s5_w_glu[l].astype(BF16), s5_b_glu[l])

        kc, vc, ks, vs, kw, vw = nsa_prep(proj, cos_nsa, sin_nsa)
        kcmp, vcmp = nsa_compress(kc, vc, nsa_pos_k[l], nsa_w1_k[l], nsa_w2_k[l], nsa_pos_v[l], nsa_w1_v[l], nsa_w2_v[l])
        o_c = nsa_attention(proj, cos_nsa, sin_nsa, kcmp, vcmp, ks, vs, kw, vw)

        q_d, k_d, v_d, gates_d = gdn_prep(proj, gdn_conv_w[l], gdn_a_log[l], gdn_dt_bias[l])
        o_d = gdn_core(q_d, k_d, v_d, gates_d, proj, gdn_o_norm[l])

        gn3 = jnp.stack([gn_mla[l], gn_s5[l], gn_nsa[l]])
        x = mix_out(o_a, o_b, o_c, o_d, gn3, cast_layer(w_out, l), x, g1, norm_post_mix[l], tm=512)

        x = ffn_fused(x, norm_pre_ffn[l], sc2, sh2, cast_layer(ffn_w_in, l), ffn_conv_w[l], cast_layer(ffn_w_out, l),
                      g2, norm_post_ffn[l])
    return x
```

```python
import functools

import jax
import jax.numpy as jnp
import numpy as np
from jax import lax
from jax.experimental import pallas as pl
from jax.experimental.pallas import tpu as pltpu

F32 = jnp.float32
BF16 = jnp.bfloat16

NORM_EPS = 1e-6
ROPE_THETA = 500000.0

MLA_HEADS, MLA_NOPE, MLA_ROPE, MLA_V, MLA_Q_RANK, MLA_KV_RANK = 4, 128, 64, 128, 384, 256
S5_GROUP, S5_STATE = 16, 64
NSA_HEADS, NSA_DIM = 4, 128
NSA_ROT = NSA_DIM // 4
CMP_BLOCK, CMP_STRIDE, SEL_BLOCK, SEL_TOPK, WINDOW = 32, 16, 64, 16, 512
GDN_HEADS, GDN_DIM, GDN_CONV, GDN_CHUNK = 4, 128, 4, 64
GROUP_W = 512

LANES = 128
SUBLANES = 8
VMEM_LIMIT_BYTES = 56 * 1024 * 1024
FFN_HALO = 16
MASKED = -1e30

PROJ_COLS = 4992
BLK_S5, BLK_NSA_Q, BLK_GDN_Q, BLK_GDN_Z = 0, 1, 2, 5
BLK_MLA = 4
BLK_NSA_KV = 5
BLK_SMALL = 36
GA_LANE, GB_LANE = 12, 16


def _cparams(*sem):
    return pltpu.CompilerParams(dimension_semantics=sem, vmem_limit_bytes=VMEM_LIMIT_BYTES)


def _rms(x, g):
    return x * lax.rsqrt(jnp.mean(x * x, axis=-1, keepdims=True) + NORM_EPS) * g


def _dot_t(a, b):
    return lax.dot_general(a, b, (((1,), (1,)), ((), ())), preferred_element_type=F32)


def _dot_b(a, b):
    return jnp.dot(a.astype(BF16), b.astype(BF16), preferred_element_type=F32)


def _cast_kernel(w_ref, o_ref):
    o_ref[...] = w_ref[0].astype(BF16)


def cast_layer(w, l, tr=256):
    _, R, C = w.shape
    tr = min(tr, R)
    tc = next((t for t in (2816, 2048, 1024) if C % t == 0), C)
    return pl.pallas_call(
        _cast_kernel,
        grid=(R // tr, C // tc),
        in_specs=[pl.BlockSpec((1, tr, tc), lambda i, j: (l, i, j))],
        out_specs=pl.BlockSpec((tr, tc), lambda i, j: (i, j)),
        out_shape=jax.ShapeDtypeStruct((R, C), BF16),
        compiler_params=_cparams("arbitrary", "arbitrary"),
        name="cast_layer",
    )(w)


def _w_in_moves():
    sizes = (MLA_Q_RANK, MLA_KV_RANK, MLA_ROPE, GROUP_W, GROUP_W) + (NSA_DIM,) * 6 + (3 * NSA_HEADS,) \
        + (GROUP_W,) * 4 + (GDN_HEADS, GDN_HEADS)
    names = ("cq", "ckv", "kpe", "u_s5", "nq", "kc", "vc", "ks", "vs", "kw", "vw", "ngate", "gq", "gk", "gv", "gz", "ga", "gb")
    src = dict(zip(names, np.concatenate([[0], np.cumsum(sizes)[:-1]]).tolist()))
    width = dict(zip(names, sizes))
    order = ("u_s5", "nq", "gq", "gk", "gv", "gz", "cq", "ckv", "kpe", None, "kc", "vc", "ks", "vs", "kw", "vw", "ngate", "ga", "gb")
    moves, dst = [], 0
    for n in order:
        if n is None:
            dst += LANES - MLA_ROPE
            continue
        moves.append((src[n], dst, width[n]))
        dst += width[n]
    return moves


def _permute_kernel(w_ref, o_ref):
    o_ref[...] = jnp.zeros_like(o_ref)
    for s, d, w in _w_in_moves():
        o_ref[:, d:d + w] = w_ref[0, :, s:s + w].astype(BF16)


def permute_w_in(w_in, l, tr=256):
    _, D, C = w_in.shape
    tr = min(tr, D)
    return pl.pallas_call(
        _permute_kernel,
        grid=(D // tr,),
        in_specs=[pl.BlockSpec((1, tr, C), lambda i: (l, i, 0))],
        out_specs=pl.BlockSpec((tr, PROJ_COLS), lambda i: (i, 0)),
        out_shape=jax.ShapeDtypeStruct((D, PROJ_COLS), BF16),
        compiler_params=_cparams("arbitrary"),
        name="permute_w_in",
    )(w_in)


def _ada_kernel(c_ref, w_ref, b_ref, o_ref):
    cond = jax.nn.silu(c_ref[...]).astype(BF16)
    o_ref[...] = jnp.dot(cond, w_ref[0].astype(BF16), preferred_element_type=F32) + b_ref[0]


def ada_mods(c, w_ada, b_ada, l):
    B, D = c.shape
    L, _, N = w_ada.shape
    c_pad = jnp.zeros((SUBLANES, D), F32).at[:B].set(c)
    tn = 1024
    assert N % tn == 0 and B <= SUBLANES
    out = pl.pallas_call(
        _ada_kernel,
        grid=(N // tn,),
        in_specs=[pl.BlockSpec((SUBLANES, D), lambda j: (0, 0)),
                  pl.BlockSpec((1, D, tn), lambda j: (l, 0, j)),
                  pl.BlockSpec((1, 1, tn), lambda j: (l, 0, j))],
        out_specs=pl.BlockSpec((SUBLANES, tn), lambda j: (0, j)),
        out_shape=jax.ShapeDtypeStruct((SUBLANES, N), F32),
        compiler_params=_cparams("arbitrary"),
        name="ada_mods",
    )(c_pad, w_ada, b_ada.reshape(L, 1, N))
    return out[:B]


def _normmod_matmul_kernel(x_ref, g_ref, sc_ref, sh_ref, w_ref, o_ref, h_ref):
    @pl.when(pl.program_id(2) == 0)
    def _():
        h_ref[...] = (_rms(x_ref[0], g_ref[...]) * (1.0 + sc_ref[0]) + sh_ref[0]).astype(BF16)

    o_ref[0] = jnp.dot(h_ref[...], w_ref[...], preferred_element_type=F32).astype(o_ref.dtype)


def normmod_matmul(x, gain, sc, sh, w_bf16, out_dtype, tm, tn):
    B, S, D = x.shape
    N = w_bf16.shape[1]
    tm = min(tm, S)
    assert S % tm == 0 and N % tn == 0
    return pl.pallas_call(
        _normmod_matmul_kernel,
        grid=(B, S // tm, N // tn),
        in_specs=[pl.BlockSpec((1, tm, D), lambda b, i, j: (b, i, 0)),
                  pl.BlockSpec((1, D), lambda b, i, j: (0, 0)),
                  pl.BlockSpec((1, 1, D), lambda b, i, j: (b, 0, 0)),
                  pl.BlockSpec((1, 1, D), lambda b, i, j: (b, 0, 0)),
                  pl.BlockSpec((D, tn), lambda b, i, j: (0, j))],
        out_specs=pl.BlockSpec((1, tm, tn), lambda b, i, j: (b, i, j)),
        out_shape=jax.ShapeDtypeStruct((B, S, N), out_dtype),
        scratch_shapes=[pltpu.VMEM((tm, D), BF16)],
        compiler_params=_cparams("arbitrary", "arbitrary", "arbitrary"),
        name="normmod_matmul",
    )(x, gain.reshape(1, D), sc.reshape(B, 1, D), sh.reshape(B, 1, D), w_bf16)


def _mix_out_kernel(oa_ref, ob_ref, oc_ref, od_ref, gn_ref, w_ref, x_ref, gate_ref, gamma_ref, o_ref):
    gw = oa_ref.shape[-1]
    parts = (_rms(oa_ref[0], gn_ref[0:1, :]), _rms(ob_ref[0], gn_ref[1:2, :]),
             _rms(oc_ref[0], gn_ref[2:3, :]), od_ref[0])
    y = None
    for p, part in enumerate(parts):
        t = jnp.dot(part.astype(BF16), w_ref[p * gw:(p + 1) * gw, :], preferred_element_type=F32)
        y = t if y is None else y + t
    o_ref[0] = x_ref[0] + gate_ref[0] * _rms(y, gamma_ref[...])


def mix_out(o_a, o_b, o_c, o_d, gn3, w_out_bf16, x, gate, gamma, tm):
    B, S, D = x.shape
    gw = o_a.shape[-1]
    tm = min(tm, S)
    grp = pl.BlockSpec((1, tm, gw), lambda b, i: (b, i, 0))
    return pl.pallas_call(
        _mix_out_kernel,
        grid=(B, S // tm),
        in_specs=[grp, grp, grp, grp,
                  pl.BlockSpec((3, gw), lambda b, i: (0, 0)),
                  pl.BlockSpec((4 * gw, D), lambda b, i: (0, 0)),
                  pl.BlockSpec((1, tm, D), lambda b, i: (b, i, 0)),
                  pl.BlockSpec((1, 1, D), lambda b, i: (b, 0, 0)),
                  pl.BlockSpec((1, D), lambda b, i: (0, 0))],
        out_specs=pl.BlockSpec((1, tm, D), lambda b, i: (b, i, 0)),
        out_shape=jax.ShapeDtypeStruct((B, S, D), F32),
        compiler_params=_cparams("arbitrary", "arbitrary"),
        name="mix_out",
    )(o_a, o_b, o_c, o_d, gn3, w_out_bf16, x, gate.reshape(B, 1, D), gamma.reshape(1, D))


def _ffn_kernel(x_ref, xh_ref, gain_ref, sc_ref, sh_ref, wg_ref, wv_ref, cw_ref, wo_ref, g2_ref, gamma_ref, o_ref,
                h_ref, acc_ref):
    i, k = pl.program_id(1), pl.program_id(2)

    @pl.when(k == 0)
    def _():
        mod = lambda t: (_rms(t, gain_ref[...]) * (1.0 + sc_ref[0]) + sh_ref[0]).astype(BF16)
        h_ref[0:FFN_HALO, :] = mod(xh_ref[0])
        h_ref[FFN_HALO:, :] = mod(x_ref[0])
        acc_ref[...] = jnp.zeros_like(acc_ref)

    gate = jnp.dot(h_ref[...], wg_ref[...], preferred_element_type=F32)
    val = jnp.dot(h_ref[FFN_HALO:, :], wv_ref[...], preferred_element_type=F32)
    row = lax.broadcasted_iota(jnp.int32, gate.shape, 0)
    gate = jnp.where((row >= FFN_HALO) | (i > 0), gate, 0.0)
    prev1 = pltpu.roll(gate, 1, axis=0)[FFN_HALO:, :]
    prev2 = pltpu.roll(gate, 2, axis=0)[FFN_HALO:, :]
    conv = cw_ref[0:1, :] * prev2 + cw_ref[1:2, :] * prev1 + cw_ref[2:3, :] * gate[FFN_HALO:, :]
    a = (jax.nn.gelu(conv) * val).astype(BF16)
    acc_ref[...] += jnp.dot(a, wo_ref[...], preferred_element_type=F32)

    @pl.when(k == pl.num_programs(2) - 1)
    def _():
        o_ref[0] = x_ref[0] + g2_ref[0] * _rms(acc_ref[...], gamma_ref[...])


def ffn_fused(x, gain, sc, sh, w_in_bf16, conv_w, w_out_bf16, g2, gamma, tm=512, tf=512):
    B, S, D = x.shape
    F = w_out_bf16.shape[0]
    tm = min(tm, S)
    nk = F // tf
    assert F % tf == 0 and S % tm == 0 and tm % FFN_HALO == 0
    rb = tm // FFN_HALO
    vec = pl.BlockSpec((1, D), lambda b, i, k: (0, 0))
    bvec = pl.BlockSpec((1, 1, D), lambda b, i, k: (b, 0, 0))
    return pl.pallas_call(
        _ffn_kernel,
        grid=(B, S // tm, nk),
        in_specs=[pl.BlockSpec((1, tm, D), lambda b, i, k: (b, i, 0)),
                  pl.BlockSpec((1, FFN_HALO, D), lambda b, i, k: (b, jnp.maximum(i * rb - 1, 0), 0)),
                  vec, bvec, bvec,
                  pl.BlockSpec((D, tf), lambda b, i, k: (0, k)),
                  pl.BlockSpec((D, tf), lambda b, i, k: (0, k + nk)),
                  pl.BlockSpec((3, tf), lambda b, i, k: (0, k)),
                  pl.BlockSpec((tf, D), lambda b, i, k: (k, 0)),
                  bvec, vec],
        out_specs=pl.BlockSpec((1, tm, D), lambda b, i, k: (b, i, 0)),
        out_shape=jax.ShapeDtypeStruct((B, S, D), F32),
        scratch_shapes=[pltpu.VMEM((tm + FFN_HALO, D), BF16), pltpu.VMEM((tm, D), F32)],
        compiler_params=_cparams("arbitrary", "arbitrary", "arbitrary"),
        name="ffn_fused",
    )(x, x, gain.reshape(1, D), sc.reshape(B, 1, D), sh.reshape(B, 1, D), w_in_bf16, w_in_bf16, conv_w, w_out_bf16,
      g2.reshape(B, 1, D), gamma.reshape(1, D))


def _rope_table_kernel(ang_ref, cos_ref, sin_ref, *, half):
    ang = ang_ref[0]
    lane = lax.broadcasted_iota(jnp.int32, ang.shape, 1)
    cos_ref[0] = jnp.where(lane < 2 * half, jnp.cos(ang), 1.0)
    sin_ref[0] = jnp.where(lane < half, -jnp.sin(ang), jnp.where(lane < 2 * half, jnp.sin(ang), 0.0))


def rope_tables(pos, half, tm=512):
    B, S = pos.shape
    inv_freq = ROPE_THETA ** (-jnp.arange(half, dtype=F32) / half)
    row = jnp.concatenate([inv_freq, inv_freq, jnp.zeros((LANES - 2 * half,), F32)])
    ang = pos.astype(F32)[:, :, None] * row
    tm = min(tm, S)
    blk = pl.BlockSpec((1, tm, LANES), lambda b, i: (b, i, 0))
    return pl.pallas_call(
        functools.partial(_rope_table_kernel, half=half),
        grid=(B, S // tm),
        in_specs=[blk], out_specs=[blk, blk],
        out_shape=[jax.ShapeDtypeStruct((B, S, LANES), F32)] * 2,
        compiler_params=_cparams("arbitrary", "arbitrary"),
        name="rope_tables",
    )(ang)


def _rope(x, cos, sin, half):
    lane = lax.broadcasted_iota(jnp.int32, x.shape, 1)
    swapped = jnp.where(lane < half, pltpu.roll(x, LANES - half, axis=1), pltpu.roll(x, half, axis=1))
    return x * cos + swapped * sin


def _online_step(s, bias, v, carry):
    m, l, acc = carry
    h, tq, tk = s.shape
    if bias is not None:
        s = s + bias[None]
    m_new = jnp.maximum(m, jnp.max(s, axis=-1, keepdims=True))
    p = jnp.exp(s - m_new)
    alpha = jnp.exp(m - m_new)
    l = alpha * l + jnp.sum(p, axis=-1, keepdims=True)
    if v.ndim == 3:
        pv = jnp.einsum('hqk,hkd->hqd', p.astype(BF16), v, preferred_element_type=F32)
    else:
        pv = jnp.dot(p.reshape(h * tq, tk).astype(BF16), v, preferred_element_type=F32).reshape(h, tq, -1)
    return m_new, l, alpha * acc + pv


def _flash_sweep(first, last, tile_fn, carry):
    n_pairs = (last - first) // 2

    def pair(j, c):
        a = tile_fn(first + 2 * j)
        b = tile_fn(first + 2 * j + 1)
        return _online_step(*b, _online_step(*a, c))

    carry = lax.fori_loop(0, n_pairs, pair, carry)
    return lax.fori_loop(first + 2 * n_pairs, last, lambda kt, c: _online_step(*tile_fn(kt), c), carry)


MLA_QK = 2 * LANES


def mla_weights(w_uq, w_ukv):
    H = MLA_HEADS
    wq = w_uq.reshape(MLA_Q_RANK, H, MLA_NOPE + MLA_ROPE)
    wq = jnp.pad(wq, ((0, 0), (0, 0), (0, MLA_QK - MLA_NOPE - MLA_ROPE))).reshape(MLA_Q_RANK, H * MLA_QK)
    return wq.astype(BF16), w_ukv.astype(BF16)


def _mla_prep_kernel(c_ref, cos_ref, sin_ref, qn_ref, kn_ref, wq_ref, wkv_ref, q_ref, k_ref, v_ref):
    H = MLA_HEADS
    c = c_ref[0]
    cos, sin = cos_ref[0], sin_ref[0]
    scale = (MLA_NOPE + MLA_ROPE) ** -0.5
    cq = _rms(c[:, :MLA_Q_RANK], qn_ref[...]).astype(BF16)
    ckv = _rms(c[:, MLA_Q_RANK:MLA_Q_RANK + MLA_KV_RANK], kn_ref[...]).astype(BF16)
    q = jnp.dot(cq, wq_ref[...], preferred_element_type=F32)
    kv = jnp.dot(ckv, wkv_ref[...], preferred_element_type=F32)
    kpe = _rope(c[:, MLA_Q_RANK + MLA_KV_RANK:], cos, sin, MLA_ROPE // 2).astype(BF16)
    for h in range(H):
        b = h * MLA_QK
        q_ref[0, :, b:b + LANES] = (q[:, b:b + LANES] * scale).astype(BF16)
        q_ref[0, :, b + LANES:b + MLA_QK] = (_rope(q[:, b + LANES:b + MLA_QK], cos, sin, MLA_ROPE // 2) * scale).astype(BF16)
        k_ref[0, :, b:b + LANES] = kv[:, b:b + LANES].astype(BF16)
        k_ref[0, :, b + LANES:b + MLA_QK] = kpe
        v_ref[0, :, h * MLA_V:(h + 1) * MLA_V] = kv[:, b + LANES:b + MLA_QK].astype(BF16)


def mla_prep(proj, cos, sin, q_norm, kv_norm, wq, wkv, tm=512):
    B, S, _ = proj.shape
    tm = min(tm, S)
    H = MLA_HEADS
    cw = MLA_Q_RANK + MLA_KV_RANK + LANES
    tab = pl.BlockSpec((1, tm, LANES), lambda b, i: (b, i, 0))
    full = lambda a: pl.BlockSpec(a.shape, lambda b, i: (0,) * a.ndim)
    qn = q_norm.reshape(1, -1)
    kn = kv_norm.reshape(1, -1)
    wide = pl.BlockSpec((1, tm, H * MLA_QK), lambda b, i: (b, i, 0))
    return pl.pallas_call(
        _mla_prep_kernel,
        grid=(B, S // tm),
        in_specs=[pl.BlockSpec((1, tm, cw), lambda b, i: (b, i, BLK_MLA)), tab, tab,
                  full(qn), full(kn), full(wq), full(wkv)],
        out_specs=[wide, wide, pl.BlockSpec((1, tm, H * MLA_V), lambda b, i: (b, i, 0))],
        out_shape=[jax.ShapeDtypeStruct((B, S, H * MLA_QK), BF16), jax.ShapeDtypeStruct((B, S, H * MLA_QK), BF16),
                   jax.ShapeDtypeStruct((B, S, H * MLA_V), BF16)],
        compiler_params=_cparams("arbitrary", "arbitrary"),
        name="mla_prep",
    )(proj, cos, sin, qn, kn, wq, wkv)


def _mla_attn_kernel(q_ref, k_ref, v_ref, o_ref, *, tk):
    tq = q_ref.shape[1]
    hp = q_ref.shape[2] // MLA_QK
    q0 = pl.program_id(2) * tq
    q = [q_ref[0, :, j * MLA_QK:(j + 1) * MLA_QK] for j in range(hp)]
    init = (jnp.full((hp, tq, 1), MASKED, F32), jnp.zeros((hp, tq, 1), F32), jnp.zeros((hp, tq, MLA_V), F32))

    def tile(masked, kt):
        k0 = pl.multiple_of(kt * tk, tk)
        s = jnp.stack([_dot_t(q[j], k_ref[0, pl.ds(k0, tk), j * MLA_QK:(j + 1) * MLA_QK]) for j in range(hp)], axis=0)
        v = jnp.stack([v_ref[0, pl.ds(k0, tk), j * MLA_V:(j + 1) * MLA_V] for j in range(hp)], axis=0)
        bias = None
        if masked:
            causal = (k0 + lax.broadcasted_iota(jnp.int32, (tq, tk), 1)) <= (q0 + lax.broadcasted_iota(jnp.int32, (tq, tk), 0))
            bias = jnp.where(causal, 0.0, MASKED)
        return s, bias, v

    n_full = q0 // tk
    n_all = (q0 + tq + tk - 1) // tk
    carry = _flash_sweep(0, n_full, functools.partial(tile, False), init)
    _, l, acc = lax.fori_loop(n_full, n_all, lambda kt, c: _online_step(*tile(True, kt), c), carry)
    o = acc / l
    for j in range(hp):
        o_ref[0, :, j * MLA_V:(j + 1) * MLA_V] = o[j]


def mla_attention(q, k, v, tq=512, tk=512, heads_per_step=2):
    B, S, _ = q.shape
    H = MLA_HEADS
    hp = heads_per_step
    tq, tk = min(tq, S), min(tk, S)
    assert H % hp == 0 and S % tq == 0 and S % tk == 0
    return pl.pallas_call(
        functools.partial(_mla_attn_kernel, tk=tk),
        grid=(B, H // hp, S // tq),
        in_specs=[pl.BlockSpec((1, tq, hp * MLA_QK), lambda b, h, i: (b, i, h)),
                  pl.BlockSpec((1, S, hp * MLA_QK), lambda b, h, i: (b, 0, h)),
                  pl.BlockSpec((1, S, hp * MLA_V), lambda b, h, i: (b, 0, h))],
        out_specs=pl.BlockSpec((1, tq, hp * MLA_V), lambda b, h, i: (b, i, h)),
        out_shape=jax.ShapeDtypeStruct((B, S, H * MLA_V), F32),
        compiler_params=_cparams("arbitrary", "arbitrary", "arbitrary"),
        name="mla_attention",
    )(q, k, v)


S5_CHUNK = 16
S5_GPB = LANES // S5_GROUP
HI = lax.Precision.HIGHEST


def s5_params(a_re, a_im, b_re, b_im, c_re, c_im, log_step, n_chunks):
    G, P = a_re.shape
    L = S5_CHUNK
    CI = b_re.shape[-1]
    CO = c_re.shape[1]
    NB = G // S5_GPB
    step = jnp.exp(log_step.astype(F32))[:, None]
    are, aim = a_re.astype(F32), a_im.astype(F32)
    mag = jnp.exp(are * step)
    lb_re, lb_im = mag * jnp.cos(aim * step), mag * jnp.sin(aim * step)
    den = are * are + aim * aim
    nr, ni = lb_re - 1.0, lb_im
    g_re = (nr * are + ni * aim) / den
    g_im = (ni * are - nr * aim) / den
    br, bi = b_re.astype(F32), b_im.astype(F32)
    bb_re = g_re[..., None] * br - g_im[..., None] * bi
    bb_im = g_re[..., None] * bi + g_im[..., None] * br

    def lam_pow(d):
        d = jnp.asarray(d, F32)[:, None, None]
        pmag = jnp.exp(are * step * d)
        return pmag * jnp.cos(aim * step * d), pmag * jnp.sin(aim * step * d)

    pw_re, pw_im = lam_pow(np.arange(L + 1))
    m_re = pw_re[..., None] * bb_re - pw_im[..., None] * bb_im
    m_im = pw_re[..., None] * bb_im + pw_im[..., None] * bb_re
    cr, ci_ = c_re.astype(F32), c_im.astype(F32)
    kd = (jnp.einsum('gop,dgpi->dgoi', cr, m_re[:L], precision=HI)
          - jnp.einsum('gop,dgpi->dgoi', ci_, m_im[:L], precision=HI))
    s_idx = np.arange(L)[:, None]
    t_idx = np.arange(L)[None, :]
    dmat = np.clip(t_idx - s_idx, 0, L - 1)
    eye = jnp.eye(S5_GPB, dtype=F32)
    kbd = jnp.einsum('dbgoi,gh->bdgiho', kd.reshape(L, NB, S5_GPB, CO, CI), eye).reshape(NB, L, LANES, LANES)
    tb = jnp.where((t_idx >= s_idx)[None, :, :, None, None], kbd[:, dmat], 0.0)
    T = tb.transpose(0, 1, 3, 2, 4).reshape(NB, L * LANES, L * LANES).astype(BF16)

    def to_p(m):
        m = m[::-1].reshape(L, NB, S5_GPB, P, CI).transpose(1, 0, 4, 2, 3).reshape(NB, L, 1, CI, S5_GPB * P)
        same = (np.arange(S5_GPB)[:, None, None] == (np.arange(S5_GPB * P) // P)[None, None, :])
        return jnp.where(same, m, 0.0).astype(BF16).reshape(NB, L * LANES, S5_GPB * P)

    q_re = cr[None] * pw_re[1:, :, None, :] - ci_[None] * pw_im[1:, :, None, :]
    q_im = -(cr[None] * pw_im[1:, :, None, :] + ci_[None] * pw_re[1:, :, None, :])

    def to_q(m):
        m = m.reshape(L, NB, S5_GPB, CO, P).transpose(1, 2, 4, 0, 3).reshape(NB, S5_GPB, P, L, 1, CO)
        same = (np.arange(S5_GPB)[:, None, None, None, None] == np.arange(S5_GPB)[None, None, None, :, None])
        return jnp.where(same, m, 0.0).astype(BF16).reshape(NB, S5_GPB * P, L * LANES)

    levels = max(1, int(np.ceil(np.log2(n_chunks))))
    lv_re, lv_im = lam_pow(L * 2 ** np.arange(levels))
    lam_lv = jnp.concatenate([lv_re.reshape(levels, NB, S5_GPB * P), lv_im.reshape(levels, NB, S5_GPB * P)], axis=0)
    lam_lv = lam_lv.transpose(1, 0, 2)
    return dict(T=T, p_re=to_p(m_re[:L]), p_im=to_p(m_im[:L]), q_re=to_q(q_re), q_im=to_q(q_im), lam=lam_lv)


def _s5_kernel(u_ref, t_ref, pre_ref, pim_ref, qre_ref, qim_ref, lam_ref, y_ref):
    L = S5_CHUNK
    nc = u_ref.shape[1] // L
    u = jnp.concatenate([u_ref[0, pl.ds(s, nc, stride=L), :].astype(BF16) for s in range(L)], axis=1)
    xr = jnp.dot(u, pre_ref[0, 0], preferred_element_type=F32)
    xi = jnp.dot(u, pim_ref[0, 0], preferred_element_type=F32)
    lam = lam_ref[0, 0]
    levels = lam.shape[0] // 2
    row = lax.broadcasted_iota(jnp.int32, xr.shape, 0)
    for k in range(levels):
        sh = 2 ** k
        if sh >= nc:
            break
        lr, li = lam[k:k + 1, :], lam[levels + k:levels + k + 1, :]
        sr = jnp.where(row >= sh, pltpu.roll(xr, sh, axis=0), 0.0)
        si = jnp.where(row >= sh, pltpu.roll(xi, sh, axis=0), 0.0)
        xr, xi = xr + lr * sr - li * si, xi + lr * si + li * sr
    pr = jnp.where(row >= 1, pltpu.roll(xr, 1, axis=0), 0.0).astype(BF16)
    pi = jnp.where(row >= 1, pltpu.roll(xi, 1, axis=0), 0.0).astype(BF16)
    for t in range(0, L, 2):
        rows, cols = (t + 2) * LANES, slice(t * LANES, (t + 2) * LANES)
        y = (jnp.dot(u[:, :rows], t_ref[0, 0, :rows, cols], preferred_element_type=F32)
             + jnp.dot(pr, qre_ref[0, 0, :, cols], preferred_element_type=F32)
             + jnp.dot(pi, qim_ref[0, 0, :, cols], preferred_element_type=F32))
        y_ref[0, pl.ds(t, nc, stride=L), :] = y[:, :LANES]
        y_ref[0, pl.ds(t + 1, nc, stride=L), :] = y[:, LANES:]


def s5_scan(proj, prm, l):
    B, S, _ = proj.shape
    L = S5_CHUNK
    NB = GROUP_W // LANES
    W = L * LANES
    ns = S5_GPB * S5_STATE
    op = lambda r, c: pl.BlockSpec((1, 1, r, c), lambda n, b: (l, n, 0, 0))
    return pl.pallas_call(
        _s5_kernel,
        grid=(NB, B),
        in_specs=[pl.BlockSpec((1, S, LANES), lambda n, b: (b, 0, BLK_S5 * NB + n)),
                  op(W, W), op(W, ns), op(W, ns), op(ns, W), op(ns, W), op(prm["lam"].shape[2], ns)],
        out_specs=pl.BlockSpec((1, S, LANES), lambda n, b: (b, 0, n)),
        out_shape=jax.ShapeDtypeStruct((B, S, GROUP_W), F32),
        compiler_params=_cparams("arbitrary", "arbitrary"),
        name="s5_scan",
    )(proj, prm["T"], prm["p_re"], prm["p_im"], prm["q_re"], prm["q_im"], prm["lam"])


def _s5_post_kernel(y_ref, u_ref, d_ref, w_ref, b_ref, o_ref):
    y = jax.nn.gelu(y_ref[0] + d_ref[...] * u_ref[0])
    z = jnp.dot(y.astype(BF16), w_ref[...], preferred_element_type=F32) + b_ref[...]
    o_ref[0] = y * jax.nn.sigmoid(z)


def s5_post(y, proj, d_skip, w_glu_bf16, b_glu, tm=512):
    B, S, CH = y.shape
    tm = min(tm, S)
    blk = pl.BlockSpec((1, tm, CH), lambda b, i: (b, i, 0))
    vec = pl.BlockSpec((1, CH), lambda b, i: (0, 0))
    return pl.pallas_call(
        _s5_post_kernel,
        grid=(B, S // tm),
        in_specs=[blk, pl.BlockSpec((1, tm, CH), lambda b, i: (b, i, BLK_S5)), vec,
                  pl.BlockSpec((CH, CH), lambda b, i: (0, 0)), vec],
        out_specs=blk,
        out_shape=jax.ShapeDtypeStruct((B, S, CH), F32),
        compiler_params=_cparams("arbitrary", "arbitrary"),
        name="s5_post",
    )(y, proj, d_skip.reshape(1, CH), w_glu_bf16, b_glu.reshape(1, CH))


def _nsa_prep_kernel(kv_ref, cos_ref, sin_ref, kc_ref, vc_ref, ks_ref, vs_ref, kw_ref, vw_ref):
    cos, sin = cos_ref[0], sin_ref[0]
    outs = (kc_ref, vc_ref, ks_ref, vs_ref, kw_ref, vw_ref)
    for n, o_ref in enumerate(outs):
        x = kv_ref[0, :, n * NSA_DIM:(n + 1) * NSA_DIM]
        if n % 2 == 0:
            x = _rope(x, cos, sin, NSA_ROT // 2)
        o_ref[0] = x.astype(BF16)


def nsa_prep(proj, cos, sin, tm=512):
    B, S, _ = proj.shape
    tm = min(tm, S)
    one = pl.BlockSpec((1, tm, NSA_DIM), lambda b, i: (b, i, 0))
    return pl.pallas_call(
        _nsa_prep_kernel,
        grid=(B, S // tm),
        in_specs=[pl.BlockSpec((1, tm, 6 * NSA_DIM), lambda b, i: (b, i, BLK_NSA_KV)), one, one],
        out_specs=[one] * 6,
        out_shape=[jax.ShapeDtypeStruct((B, S, NSA_DIM), BF16)] * 6,
        compiler_params=_cparams("arbitrary", "arbitrary"),
        name="nsa_prep",
    )(proj, cos, sin)


def _nsa_compress_kernel(xk_ref, xv_ref, pk_ref, w1k_ref, w2k_ref, pv_ref, w1v_ref, w2v_ref, ok_ref, ov_ref):
    for x_ref, p_ref, w1_ref, w2_ref, o_ref in ((xk_ref, pk_ref, w1k_ref, w2k_ref, ok_ref),
                                                (xv_ref, pv_ref, w1v_ref, w2v_ref, ov_ref)):
        x = x_ref[0]
        half = x.shape[1]
        a = jnp.dot(x, w1_ref[0:half, :], preferred_element_type=F32)
        b = jnp.dot(x, w1_ref[half:, :], preferred_element_type=F32)
        c = jnp.dot(p_ref[...], w1_ref[...], preferred_element_type=F32)[0:1, :]
        pre = a + pltpu.roll(b, x.shape[0] - 1, axis=0) + c
        h = jax.nn.gelu(pre)
        o_ref[0] = jnp.dot(h.astype(BF16), w2_ref[...], preferred_element_type=F32).astype(BF16)


def nsa_compress(kc, vc, pos_k, w1k, w2k, pos_v, w1v, w2v):
    B, S, D = kc.shape
    nseg = S // CMP_STRIDE
    seg_w = CMP_STRIDE * D
    xk = kc.reshape(B, nseg, seg_w)
    xv = vc.reshape(B, nseg, seg_w)

    def posrow(p):
        return jnp.broadcast_to(p.reshape(1, CMP_BLOCK * D), (SUBLANES, CMP_BLOCK * D)).astype(BF16)

    xs = pl.BlockSpec((1, nseg, seg_w), lambda b: (b, 0, 0))
    ps = pl.BlockSpec((SUBLANES, CMP_BLOCK * D), lambda b: (0, 0))
    w1s = pl.BlockSpec((CMP_BLOCK * D, D), lambda b: (0, 0))
    w2s = pl.BlockSpec((D, D), lambda b: (0, 0))
    os_ = pl.BlockSpec((1, nseg, D), lambda b: (b, 0, 0))
    return pl.pallas_call(
        _nsa_compress_kernel,
        grid=(B,),
        in_specs=[xs, xs, ps, w1s, w2s, ps, w1s, w2s],
        out_specs=[os_, os_],
        out_shape=[jax.ShapeDtypeStruct((B, nseg, D), BF16)] * 2,
        compiler_params=_cparams("arbitrary"),
        name="nsa_compress",
    )(xk, xv, posrow(pos_k), w1k.astype(BF16), w2k.astype(BF16), posrow(pos_v), w1v.astype(BF16), w2v.astype(BF16))


def _nsa_attn_kernel(q_ref, gate_ref, cos_ref, sin_ref, kc_ref, vc_ref, ks_ref, vs_ref, kw_ref, vw_ref, o_ref,
                     *, tk_sel):
    H, D = NSA_HEADS, NSA_DIM
    tq = q_ref.shape[1]
    S = ks_ref.shape[1]
    n_cmp = kc_ref.shape[1]
    n_sel = S // SEL_BLOCK
    q0 = pl.program_id(1) * tq
    scale = D ** -0.5
    cos, sin = cos_ref[0], sin_ref[0]
    q = jnp.concatenate(
        [(_rope(q_ref[0, :, h * D:(h + 1) * D], cos, sin, NSA_ROT // 2) * scale).astype(BF16) for h in range(H)],
        axis=0)

    n_c = lax.broadcasted_iota(jnp.int32, (n_cmp, tq), 0)
    t_c = q0 + lax.broadcasted_iota(jnp.int32, (n_cmp, tq), 1)
    valid1 = (t_c >= n_c * CMP_STRIDE + (CMP_BLOCK - 1)) & (n_c < n_cmp - 1)
    valid = jnp.concatenate([valid1] * H, axis=1)
    s = jnp.where(valid, _dot_t(kc_ref[0], q), MASKED)
    e = jnp.exp(s - jnp.max(s, axis=0, keepdims=True))
    p = jnp.where(valid, e / jnp.sum(e, axis=0, keepdims=True), 0.0)
    o_cmp = lax.dot_general(p.astype(BF16), vc_ref[0], (((0,), (0,)), ((), ())),
                            preferred_element_type=F32).reshape(H, tq, D)

    psum = p[:, 0:tq]
    for h in range(1, H):
        psum = psum + p[:, h * tq:(h + 1) * tq]
    j_o = lax.broadcasted_iota(jnp.int32, (n_sel, n_cmp), 0) * SEL_BLOCK
    n_o = lax.broadcasted_iota(jnp.int32, (n_sel, n_cmp), 1) * CMP_STRIDE
    overlap = jnp.where((n_o < j_o + SEL_BLOCK) & (n_o + CMP_BLOCK > j_o), 1.0, 0.0).astype(BF16)
    p_hi = psum.astype(BF16)
    p_lo = (psum - p_hi.astype(F32)).astype(BF16)
    imp = (jnp.dot(overlap, p_hi, preferred_element_type=F32) + jnp.dot(overlap, p_lo, preferred_element_type=F32))
    j_i = lax.broadcasted_iota(jnp.int32, (n_sel, tq), 0)
    t_blk = (q0 + lax.broadcasted_iota(jnp.int32, (n_sel, tq), 1)) // SEL_BLOCK
    forced = (j_i == 0) | (j_i == t_blk) | (j_i == t_blk - 1)
    imp = jnp.where(forced, 1e6, jnp.where(j_i > t_blk, -1e6, imp))
    j_s = j_i.astype(F32)
    sel = jnp.zeros((n_sel, tq), F32)
    for _ in range(min(SEL_TOPK, n_sel)):
        mx = jnp.max(imp, axis=0, keepdims=True)
        first = jnp.min(jnp.where(imp == mx, j_s, float(n_sel)), axis=0, keepdims=True)
        hit = j_s == first
        sel = jnp.where(hit, 1.0, sel)
        imp = jnp.where(hit, -jnp.inf, imp)
    sel_b = sel.astype(BF16)

    init = (jnp.full((H, tq, 1), MASKED, F32), jnp.zeros((H, tq, 1), F32), jnp.zeros((H, tq, D), F32))

    def sel_tile(kt):
        k0 = pl.multiple_of(kt * tk_sel, tk_sel)
        sc = _dot_t(q, ks_ref[0, pl.ds(k0, tk_sel), :]).reshape(H, tq, tk_sel)
        j_e = lax.broadcasted_iota(jnp.int32, (n_sel, tk_sel), 0)
        c_e = (k0 + lax.broadcasted_iota(jnp.int32, (n_sel, tk_sel), 1)) // SEL_BLOCK
        expand = jnp.where(j_e == c_e, 1.0, 0.0).astype(BF16)
        chosen = lax.dot_general(sel_b, expand, (((0,), (0,)), ((), ())), preferred_element_type=F32)
        kpos = k0 + lax.broadcasted_iota(jnp.int32, (tq, tk_sel), 1)
        tpos = q0 + lax.broadcasted_iota(jnp.int32, (tq, tk_sel), 0)
        bias = jnp.where(kpos <= tpos, (chosen - 1.0) * -MASKED, MASKED)
        return sc, bias, vs_ref[0, pl.ds(k0, tk_sel), :]

    n_kt = (q0 + tq + tk_sel - 1) // tk_sel
    _, l_s, acc_s = _flash_sweep(0, n_kt, sel_tile, init)
    o_slc = acc_s / l_s

    span = WINDOW + tq
    w0 = pl.multiple_of(jnp.maximum(q0 - WINDOW, 0), tq)
    sw = _dot_t(q, kw_ref[0, pl.ds(w0, span), :]).reshape(H, tq, span)
    rel = (q0 + lax.broadcasted_iota(jnp.int32, (tq, span), 0)) - (w0 + lax.broadcasted_iota(jnp.int32, (tq, span), 1))
    in_win = ((rel >= 0) & (rel < WINDOW))[None]
    sw = jnp.where(in_win, sw, MASKED)
    ew = jnp.where(in_win, jnp.exp(sw - jnp.max(sw, axis=-1, keepdims=True)), 0.0)
    o_win = jnp.dot(ew.reshape(H * tq, span).astype(BF16), vw_ref[0, pl.ds(w0, span), :],
                    preferred_element_type=F32).reshape(H, tq, D) / jnp.sum(ew, axis=-1, keepdims=True)

    gates = jax.nn.sigmoid(gate_ref[0])
    for h in range(H):
        o_ref[0, :, h * D:(h + 1) * D] = (gates[:, h:h + 1] * o_cmp[h] + gates[:, H + h:H + h + 1] * o_slc[h]
                                          + gates[:, 2 * H + h:2 * H + h + 1] * o_win[h])


def nsa_attention(proj, cos, sin, kcmp, vcmp, ks, vs, kw, vw, tq=128):
    B, S, _ = proj.shape
    D = NSA_DIM
    n_cmp = kcmp.shape[1]
    assert S % tq == 0 and WINDOW % tq == 0 and S >= WINDOW + tq
    tk_sel = min(512, S)
    tile = pl.BlockSpec((1, tq, D), lambda b, i: (b, i, 0))
    seq = pl.BlockSpec((1, S, D), lambda b, i: (b, 0, 0))
    cmp_ = pl.BlockSpec((1, n_cmp, D), lambda b, i: (b, 0, 0))
    return pl.pallas_call(
        functools.partial(_nsa_attn_kernel, tk_sel=tk_sel),
        grid=(B, S // tq),
        in_specs=[pl.BlockSpec((1, tq, NSA_HEADS * D), lambda b, i: (b, i, BLK_NSA_Q)),
                  pl.BlockSpec((1, tq, LANES), lambda b, i: (b, i, BLK_SMALL)),
                  tile, tile, cmp_, cmp_, seq, seq, seq, seq],
        out_specs=pl.BlockSpec((1, tq, NSA_HEADS * D), lambda b, i: (b, i, 0)),
        out_shape=jax.ShapeDtypeStruct((B, S, NSA_HEADS * D), F32),
        compiler_params=_cparams("arbitrary", "arbitrary"),
        name="nsa_attention",
    )(proj, proj, cos, sin, kcmp, vcmp, ks, vs, kw, vw)


GDN_W = GDN_HEADS * GDN_DIM


def _causal_conv_silu(x, halo, w_ref, c0):
    taps = w_ref.shape[0]
    row = lax.broadcasted_iota(jnp.int32, x.shape, 0)
    acc = x * w_ref[taps - 1:taps, c0:c0 + x.shape[1]]
    for d in range(1, taps):
        shifted = pltpu.roll(x, d, axis=0)
        for r in range(d):
            shifted = jnp.where(row == r, halo[SUBLANES - d + r:SUBLANES - d + r + 1, :], shifted)
        acc = acc + shifted * w_ref[taps - 1 - d:taps - d, c0:c0 + x.shape[1]]
    return acc * jax.nn.sigmoid(acc)


def _gdn_prep_kernel(q_ref, k_ref, v_ref, hq_ref, hk_ref, hv_ref, sm_ref, w_ref, a_ref, dt_ref,
                     qo_ref, ko_ref, vo_ref, go_ref):
    first = pl.program_id(1) == 0
    for n, (x_ref, h_ref, o_ref) in enumerate(((q_ref, hq_ref, qo_ref), (k_ref, hk_ref, ko_ref), (v_ref, hv_ref, vo_ref))):
        halo = jnp.where(first, 0.0, h_ref[0])
        y = _causal_conv_silu(x_ref[0], halo, w_ref, n * GDN_W)
        for h in range(GDN_HEADS):
            yh = y[:, h * GDN_DIM:(h + 1) * GDN_DIM]
            if n < 2:
                yh = yh * lax.rsqrt(jnp.sum(yh * yh, axis=-1, keepdims=True) + 1e-6)
            if n == 0:
                yh = yh * GDN_DIM ** -0.5
            o_ref[0, :, h * GDN_DIM:(h + 1) * GDN_DIM] = yh
    sm = sm_ref[0]
    lane = lax.broadcasted_iota(jnp.int32, sm.shape, 1)
    g = -a_ref[...] * jax.nn.softplus(sm + dt_ref[...])
    go_ref[0] = jnp.where((lane >= GA_LANE) & (lane < GA_LANE + GDN_HEADS), g, jax.nn.sigmoid(sm))


def gdn_prep(proj, conv_w, a_log, dt_bias, tm=256):
    B, S, _ = proj.shape
    tm = min(tm, S)
    rb = tm // SUBLANES
    a_row = jnp.zeros((1, LANES), F32).at[0, GA_LANE:GA_LANE + GDN_HEADS].set(jnp.exp(a_log.astype(F32)))
    dt_row = jnp.zeros((1, LANES), F32).at[0, GA_LANE:GA_LANE + GDN_HEADS].set(dt_bias.astype(F32))

    def main(n):
        return pl.BlockSpec((1, tm, GDN_W), lambda b, i: (b, i, BLK_GDN_Q + n))

    def halo(n):
        return pl.BlockSpec((1, SUBLANES, GDN_W), lambda b, i: (b, jnp.maximum(i * rb - 1, 0), BLK_GDN_Q + n))

    out = pl.BlockSpec((1, tm, GDN_W), lambda b, i: (b, i, 0))
    sm = pl.BlockSpec((1, tm, LANES), lambda b, i: (b, i, BLK_SMALL))
    row = pl.BlockSpec((1, LANES), lambda b, i: (0, 0))
    return pl.pallas_call(
        _gdn_prep_kernel,
        grid=(B, S // tm),
        in_specs=[main(0), main(1), main(2), halo(0), halo(1), halo(2), sm,
                  pl.BlockSpec(conv_w.shape, lambda b, i: (0, 0)), row, row],
        out_specs=[out, out, out, pl.BlockSpec((1, tm, LANES), lambda b, i: (b, i, 0))],
        out_shape=[jax.ShapeDtypeStruct((B, S, GDN_W), F32)] * 3 + [jax.ShapeDtypeStruct((B, S, LANES), F32)],
        compiler_params=_cparams("arbitrary", "arbitrary"),
        name="gdn_prep",
    )(proj, proj, proj, proj, proj, proj, proj, conv_w, a_row, dt_row)


def _split3(x):
    hi = x.astype(BF16)
    r = x - hi.astype(F32)
    mid = r.astype(BF16)
    return hi, mid, (r - mid.astype(F32)).astype(BF16)


def _bmm(a, b):
    return jnp.einsum('nik,nkj->nij', a.astype(BF16), b.astype(BF16), preferred_element_type=F32)


def _bmm_t(a, b):
    return jnp.einsum('nik,njk->nij', a.astype(BF16), b.astype(BF16), preferred_element_type=F32)


def _gdn_core_kernel(q_ref, k_ref, v_ref, g_ref, grow_ref, z_ref, on_ref, o_ref, st_ref):
    C, D, H = GDN_CHUNK, GDN_DIM, GDN_HEADS
    tc = q_ref.shape[1]
    nc = tc // C

    @pl.when(pl.program_id(1) == 0)
    def _():
        st_ref[...] = jnp.zeros_like(st_ref)

    def per_head(ref):
        x = ref[0]
        return jnp.stack([x[c * C:(c + 1) * C, h * D:(h + 1) * D] for c in range(nc) for h in range(H)], axis=0)

    ri = lax.broadcasted_iota(jnp.int32, (C, C), 0)
    ci = lax.broadcasted_iota(jnp.int32, (C, C), 1)
    incl = (ri >= ci)[None]
    strict = (ri > ci)[None]
    eye = jnp.where(ri == ci, 1.0, 0.0)[None]
    upper_ones = jnp.where(ri <= ci, 1.0, 0.0).astype(BF16)
    rt = lax.broadcasted_iota(jnp.int32, (tc, tc), 0)
    ct = lax.broadcasted_iota(jnp.int32, (tc, tc), 1)
    chunk_lower = jnp.where((rt >= ct) & (rt // C == ct // C), 1.0, 0.0).astype(BF16)

    gates = g_ref[0]
    gc_all = sum(jnp.dot(chunk_lower, p, preferred_element_type=F32) for p in _split3(gates))
    grow = grow_ref[0].reshape(nc * SUBLANES, C)
    gr_all = sum(jnp.dot(p, upper_ones, preferred_element_type=F32) for p in _split3(grow))

    def col(x, lane0):
        return jnp.stack([x[c * C:(c + 1) * C, lane0 + h:lane0 + h + 1] for c in range(nc) for h in range(H)], axis=0)

    gc = col(gc_all, GA_LANE)
    beta = col(gates, GB_LANE)
    gr = jnp.stack([gr_all[c * SUBLANES + h:c * SUBLANES + h + 1, :] for c in range(nc) for h in range(H)], axis=0)
    q, k, v = per_head(q_ref), per_head(k_ref), per_head(v_ref)

    decay = jnp.where(incl, jnp.exp(jnp.where(incl, gc - gr, 0.0)), 0.0)
    kb = k * beta
    egc = jnp.exp(gc)
    low = jnp.where(strict, _bmm_t(kb, k) * decay, 0.0)
    s_ = 1
    inv = None
    while s_ < C:
        pm = ((ri // (2 * s_) == ci // (2 * s_)) & (ri % (2 * s_) >= s_) & (ci % (2 * s_) < s_))[None]
        off = jnp.where(pm, low, 0.0)
        inv = eye - off if inv is None else inv - _bmm(inv, _bmm(off, inv))
        s_ *= 2
    x = _bmm(inv, jnp.concatenate([v * beta, kb * egc], axis=2))
    u, w = x[:, :, :D], x[:, :, D:]
    intra = jnp.where(incl, _bmm_t(q, k) * decay, 0.0)
    g_last = gc[:, C - 1:C, :]
    q_dec = q * egc
    k_dec = k * jnp.exp(g_last - gc)
    e_last = jnp.exp(g_last)

    st = [st_ref[h] for h in range(H)]
    for c in range(nc):
        for h in range(H):
            i = c * H + h
            v_new = u[i] - _dot_b(w[i], st[h])
            o = _dot_b(q_dec[i], st[h]) + _dot_b(intra[i], v_new)
            st[h] = st[h] * e_last[i] + lax.dot_general(
                k_dec[i].astype(BF16), v_new.astype(BF16), (((0,), (0,)), ((), ())), preferred_element_type=F32)
            zh = z_ref[0, c * C:(c + 1) * C, h * D:(h + 1) * D]
            o_ref[0, c * C:(c + 1) * C, h * D:(h + 1) * D] = _rms(o, on_ref[...]) * (zh * jax.nn.sigmoid(zh))
    for h in range(H):
        st_ref[h] = st[h]


def gdn_core(q, k, v, gates, proj, o_norm, tc=512):
    B, S, _ = q.shape
    C = GDN_CHUNK
    tc = min(tc, S)
    assert S % tc == 0 and tc % C == 0
    g_rows = gates[..., GA_LANE:GA_LANE + GDN_HEADS].reshape(B, S // C, C, GDN_HEADS).transpose(0, 1, 3, 2)
    g_rows = jnp.pad(g_rows, ((0, 0), (0, 0), (0, SUBLANES - GDN_HEADS), (0, 0)))
    blk = pl.BlockSpec((1, tc, GDN_W), lambda b, i: (b, i, 0))
    return pl.pallas_call(
        _gdn_core_kernel,
        grid=(B, S // tc),
        in_specs=[blk, blk, blk,
                  pl.BlockSpec((1, tc, LANES), lambda b, i: (b, i, 0)),
                  pl.BlockSpec((1, tc // C, SUBLANES, C), lambda b, i: (b, i, 0, 0)),
                  pl.BlockSpec((1, tc, GDN_W), lambda b, i: (b, i, BLK_GDN_Z)),
                  pl.BlockSpec((1, GDN_DIM), lambda b, i: (0, 0))],
        out_specs=blk,
        out_shape=jax.ShapeDtypeStruct((B, S, GDN_W), F32),
        scratch_shapes=[pltpu.VMEM((GDN_HEADS, GDN_DIM, GDN_DIM), F32)],
        compiler_params=_cparams("arbitrary", "arbitrary"),
        name="gdn_core",
    )(q, k, v, gates, g_rows, proj, o_norm.reshape(1, GDN_DIM))


def kernel(x, c, positions, w_ada, b_ada, norm_pre_mix, norm_post_mix, norm_pre_ffn, norm_post_ffn, w_in, w_out, gn_mla, gn_s5, gn_nsa, mla_q_norm, mla_w_uq, mla_kv_norm, mla_w_ukv, s5_a_re, s5_a_im, s5_b_re, s5_b_im, s5_c_re, s5_c_im, s5_d, s5_log_step, s5_w_glu, s5_b_glu, nsa_pos_k, nsa_w1_k, nsa_w2_k, nsa_pos_v, nsa_w1_v, nsa_w2_v, gdn_conv_w, gdn_a_log, gdn_dt_bias, gdn_o_norm, ffn_w_in, ffn_conv_w, ffn_w_out):
    depth = w_in.shape[0]
    cos_mla, sin_mla = rope_tables(positions, MLA_ROPE // 2)
    cos_nsa, sin_nsa = rope_tables(positions, NSA_ROT // 2)
    s5p = jax.vmap(functools.partial(s5_params, n_chunks=x.shape[1] // S5_CHUNK))(
        s5_a_re, s5_a_im, s5_b_re, s5_b_im, s5_c_re, s5_c_im, s5_log_step)
    for l in range(depth):
        mods = ada_mods(c, w_ada, b_ada, l)
        sh1, sc1, g1, sh2, sc2, g2 = jnp.split(mods, 6, axis=-1)

        proj = normmod_matmul(x, norm_pre_mix[l], sc1, sh1, permute_w_in(w_in, l), F32, tm=512, tn=PROJ_COLS // 3)

        wq, wkv = mla_weights(mla_w_uq[l], mla_w_ukv[l])
        q_a, k_a, v_a = mla_prep(proj, cos_mla, sin_mla, mla_q_norm[l], mla_kv_norm[l], wq, wkv)
        o_a = mla_attention(q_a, k_a, v_a)

        y_b = s5_scan(proj, s5p, l)
        o_b = s5_post(y_b, proj, s5_d[l], s5_w_glu[l].astype(BF16), s5_b_glu[l])

        kc, vc, ks, vs, kw, vw = nsa_prep(proj, cos_nsa, sin_nsa)
        kcmp, vcmp = nsa_compress(kc, vc, nsa_pos_k[l], nsa_w1_k[l], nsa_w2_k[l], nsa_pos_v[l], nsa_w1_v[l], nsa_w2_v[l])
        o_c = nsa_attention(proj, cos_nsa, sin_nsa, kcmp, vcmp, ks, vs, kw, vw)

        q_d, k_d, v_d, gates_d = gdn_prep(proj, gdn_conv_w[l], gdn_a_log[l], gdn_dt_bias[l])
        o_d = gdn_core(q_d, k_d, v_d, gates_d, proj, gdn_o_norm[l])

        gn3 = jnp.stack([gn_mla[l], gn_s5[l], gn_nsa[l]])
        x = mix_out(o_a, o_b, o_c, o_d, gn3, cast_layer(w_out, l), x, g1, norm_post_mix[l], tm=512)

        x = ffn_fused(x, norm_pre_ffn[l], sc2, sh2, cast_layer(ffn_w_in, l), ffn_conv_w[l], cast_layer(ffn_w_out, l),
                      g2, norm_post_ffn[l])
    return x
```
